```python
import math, functools
import jax, jax.numpy as jnp
from jax import lax
import numpy as np

D_MODEL = 2048
BATCH = 8
SEQ = 2048
DEPTH = 1
DEC_BATCH = 32
DEC_SEQ = 4
PAST_LEN = 16384
PAGE_SIZE = 128

D_MIX = D_MODEL
ATTN_WIDTH = D_MIX // 2
SSM_WIDTH = D_MIX - ATTN_WIDTH
ATTN_HEAD_DIM = 128
N_ATTN_HEADS = ATTN_WIDTH // ATTN_HEAD_DIM
Q_BLOCK = 128
SB_SCALE = ATTN_HEAD_DIM ** -0.5
SB_BIAS_INIT = -8.0
SSM_HEAD_DIM = 64
N_SSM_HEADS = SSM_WIDTH // SSM_HEAD_DIM
SSM_STATE = 128
N_BC_GROUPS = 4
CONV_WIDTH = 4
CONV_DIM = SSM_WIDTH + 2 * N_BC_GROUPS * SSM_STATE
SSD_CHUNK = 128
IN_SPLITS = (ATTN_WIDTH, 2 * ATTN_WIDTH, 3 * ATTN_WIDTH, 3 * ATTN_WIDTH + SSM_WIDTH,
             3 * ATTN_WIDTH + SSM_WIDTH + CONV_DIM)
IN_DIM = 3 * ATTN_WIDTH + SSM_WIDTH + CONV_DIM + N_SSM_HEADS
N_EXPERTS = 64
N_EXPERT_GROUPS = 8
TOPK_GROUPS = 4
TOP_K = 8
D_EXPERT = D_MODEL // 4
D_SHARED = D_EXPERT
ROUTED_SCALE = 2.5
EXPERT_BLOCK = 128
ALPHA = (2 * DEPTH) ** 0.25
BETA = (8 * DEPTH) ** -0.25
LN_EPS = 1e-5

kernel_name = 'stickbreak_ssd_moe_deepnorm_step'


def layer_norm(x, g, b):
    xf = x.astype(jnp.float32)
    mu = jnp.mean(xf, axis=-1, keepdims=True)
    var = jnp.mean(jnp.square(xf - mu), axis=-1, keepdims=True)
    return ((xf - mu) * lax.rsqrt(var + LN_EPS) * g + b).astype(x.dtype)


def rms_norm(x, g, groups=1):
    xf = x.astype(jnp.float32)
    xg = xf.reshape(x.shape[:-1] + (groups, x.shape[-1] // groups))
    xg = xg * lax.rsqrt(jnp.mean(jnp.square(xg), axis=-1, keepdims=True) + LN_EPS)
    return (xg.reshape(x.shape) * g).astype(x.dtype)


def swiglu(h, wg, wu, wd):
    return (jax.nn.silu(h @ wg) * (h @ wu)) @ wd


def stick_breaking_weights(z, valid):
    log_beta = jax.nn.log_sigmoid(z)
    log_keep = jnp.where(valid, jax.nn.log_sigmoid(-z), 0.0)
    log_keep_after = lax.cumsum(log_keep, axis=z.ndim - 1, reverse=True) - log_keep
    return jnp.where(valid, jnp.exp(log_beta + log_keep_after), 0.0)


def sb_logits(q, k, sb_bias):
    z = jnp.einsum('bqhd,bkhd->bhqk', q, k).astype(jnp.float32) * SB_SCALE
    return z + sb_bias.astype(jnp.float32)[None, :, None, None]


def sb_attend_prompt(q, k, v, sb_bias):
    b, s, h, d = q.shape
    key_pos = jnp.arange(s)

    def block(i):
        qb = lax.dynamic_slice_in_dim(q, i * Q_BLOCK, Q_BLOCK, axis=1)
        z = sb_logits(qb, k, sb_bias)
        q_pos = i * Q_BLOCK + jnp.arange(Q_BLOCK)
        valid = key_pos[None, :] < q_pos[:, None]
        a = stick_breaking_weights(z, valid)
        return jnp.einsum('bhqk,bkhd->bqhd', a.astype(v.dtype), v)

    o = lax.map(block, jnp.arange(s // Q_BLOCK))
    return o.transpose(1, 0, 2, 3, 4).reshape(b, s, h, d)


def sb_attend_sample(q, k, v, sb_bias, k_past, v_past):
    t = q.shape[1]
    p = k_past.shape[1]
    z = jnp.concatenate([sb_logits(q, k_past, sb_bias), sb_logits(q, k, sb_bias)], axis=-1)
    key_pos = jnp.arange(p + t)
    q_pos = p + jnp.arange(t)
    valid = key_pos[None, :] < q_pos[:, None]
    a = stick_breaking_weights(z, valid).astype(v.dtype)
    return (jnp.einsum('bhqk,bkhd->bqhd', a[..., :p], v_past)
            + jnp.einsum('bhqk,bkhd->bqhd', a[..., p:], v))


def ssd_chunked(x, dt, A, bm, cm, h0):
    b, L, H, P = x.shape
    N = bm.shape[-1]
    Q = SSD_CHUNK if L % SSD_CHUNK == 0 else L
    nc = L // Q
    rep = H // bm.shape[2]
    bh = jnp.repeat(bm.astype(jnp.float32), rep, axis=2).reshape(b, nc, Q, H, N)
    ch = jnp.repeat(cm.astype(jnp.float32), rep, axis=2).reshape(b, nc, Q, H, N)
    xf = x.astype(jnp.float32).reshape(b, nc, Q, H, P)
    dtc = dt.reshape(b, nc, Q, H)
    acum = jnp.cumsum(dtc * A, axis=2)
    causal = jnp.tril(jnp.ones((Q, Q), dtype=bool))[None, None, :, :, None]
    diff = acum[:, :, :, None, :] - acum[:, :, None, :, :]
    decay = jnp.exp(jnp.where(causal, diff, -jnp.inf))
    cb = jnp.einsum('bcthn,bcshn->bctsh', ch, bh)
    y_intra = jnp.einsum('bctsh,bcsh,bcshp->bcthp', cb * decay, dtc, xf)
    a_last = acum[:, :, -1]
    decay_end = jnp.exp(a_last[:, :, None, :] - acum)
    s_chunk = jnp.einsum('bcsh,bcshn,bcshp->bchpn', decay_end * dtc, bh, xf)

    def step(h, inp):
        s_c, al = inp
        return h * jnp.exp(al)[:, :, None, None] + s_c, h

    h_final, h_starts = lax.scan(step, h0.astype(jnp.float32),
                                 (s_chunk.transpose(1, 0, 2, 3, 4), a_last.transpose(1, 0, 2)))
    h_starts = h_starts.transpose(1, 0, 2, 3, 4)
    y_inter = jnp.einsum('bcthn,bchpn->bcthp', ch, h_starts) * jnp.exp(acum)[..., None]
    return (y_intra + y_inter).reshape(b, L, H, P), h_final


def ssd_mixer(xbc, dt_raw, conv_prev, h0, conv_w, conv_b, dt_bias, a_log, d_skip):
    b, L, _ = xbc.shape
    xp = jnp.concatenate([conv_prev.astype(xbc.dtype), xbc], axis=1)
    new_conv = xp[:, -(CONV_WIDTH - 1):]
    xc = lax.conv_general_dilated(xp, conv_w.astype(xp.dtype)[:, None, :], window_strides=(1,),
                                  padding='VALID', dimension_numbers=('NWC', 'WIO', 'NWC'),
                                  feature_group_count=CONV_DIM)
    xc = jax.nn.silu(xc + conv_b)
    xs, bm, cm = jnp.split(xc, [SSM_WIDTH, SSM_WIDTH + N_BC_GROUPS * SSM_STATE], axis=-1)
    xs = xs.reshape(b, L, N_SSM_HEADS, SSM_HEAD_DIM)
    bm = bm.reshape(b, L, N_BC_GROUPS, SSM_STATE)
    cm = cm.reshape(b, L, N_BC_GROUPS, SSM_STATE)
    dt = jax.nn.softplus(dt_raw.astype(jnp.float32) + dt_bias.astype(jnp.float32))
    A = -jnp.exp(a_log.astype(jnp.float32))
    y, h = ssd_chunked(xs, dt, A, bm, cm, h0)
    y = y + d_skip.astype(jnp.float32)[:, None] * xs.astype(jnp.float32)
    return y.reshape(b, L, SSM_WIDTH).astype(xbc.dtype), new_conv, h.astype(xbc.dtype)


def mixer(x, attend, conv_prev, h0, w_in, sb_bias, conv_w, conv_b, dt_bias, a_log, d_skip, g_attn, g_ssd, w_out):
    b, L, _ = x.shape
    proj = x @ w_in
    q, k, v, z, xbc, dt_raw = jnp.split(proj, list(IN_SPLITS), axis=-1)
    q, k, v = (t.reshape(b, L, N_ATTN_HEADS, ATTN_HEAD_DIM) for t in (q, k, v))
    o_attn = attend(q, k, v, sb_bias).reshape(b, L, ATTN_WIDTH)
    y_ssd, new_conv, new_h = ssd_mixer(xbc, dt_raw, conv_prev, h0, conv_w, conv_b, dt_bias, a_log, d_skip)
    merged = jnp.concatenate([rms_norm(o_attn, g_attn),
                              rms_norm(y_ssd * jax.nn.silu(z), g_ssd, N_BC_GROUPS)], axis=-1)
    return merged @ w_out, k, v, new_conv, new_h


def moe_routed(xf, idx, gates, w_gate_e, w_up_e, w_down_e):
    M, D = xf.shape
    E, K, R = N_EXPERTS, TOP_K, EXPERT_BLOCK
    MK = M * K
    flat_e = idx.reshape(-1)
    flat_tok = jnp.arange(MK, dtype=jnp.int32) // K
    flat_g = gates.reshape(-1)
    order = jnp.argsort(flat_e)
    se, stok, sg = flat_e[order], flat_tok[order], flat_g[order]
    counts = jnp.bincount(flat_e, length=E)
    start = jnp.cumsum(counts) - counts
    padded = (counts + R - 1) // R * R
    pend = jnp.cumsum(padded)
    pstart = pend - padded
    dest = pstart[se] + (jnp.arange(MK) - start[se])
    nb = -(-MK // R) + E
    cap = nb * R
    tok_buf = jnp.full((cap,), M, jnp.int32).at[dest].set(stok)
    gate_buf = jnp.zeros((cap,), xf.dtype).at[dest].set(sg)
    blk_e = jnp.minimum(jnp.searchsorted(pend, jnp.arange(nb) * R, side='right'), E - 1)
    xpad = jnp.concatenate([xf, jnp.zeros((1, D), xf.dtype)], axis=0)

    def run_block(args):
        toks, e = args
        return swiglu(xpad[toks], w_gate_e[e], w_up_e[e], w_down_e[e])

    out = lax.map(run_block, (tok_buf.reshape(nb, R), blk_e)).reshape(cap, D)
    return jax.ops.segment_sum(out * gate_buf[:, None], tok_buf, num_segments=M + 1)[:M]


def moe(x, w_router, router_bias, w_gate_e, w_up_e, w_down_e, w_gate_s, w_up_s, w_down_s):
    b, L, D = x.shape
    xf = x.reshape(-1, D)
    M = xf.shape[0]
    scores = jax.nn.sigmoid((xf @ w_router).astype(jnp.float32))
    choice = scores + router_bias.astype(jnp.float32)
    grp = choice.reshape(M, N_EXPERT_GROUPS, N_EXPERTS // N_EXPERT_GROUPS)
    grp_score = lax.top_k(grp, 2)[0].sum(-1)
    top_g = lax.top_k(grp_score, TOPK_GROUPS)[1]
    gmask = (top_g[..., None] == jnp.arange(N_EXPERT_GROUPS)).any(axis=-2)
    emask = jnp.repeat(gmask, N_EXPERTS // N_EXPERT_GROUPS, axis=-1)
    idx = lax.top_k(jnp.where(emask, choice, -jnp.inf), TOP_K)[1]
    w = jnp.take_along_axis(scores, idx, axis=-1)
    w = w / jnp.sum(w, axis=-1, keepdims=True) * ROUTED_SCALE
    routed = moe_routed(xf, idx, w.astype(x.dtype), w_gate_e, w_up_e, w_down_e)
    shared = swiglu(xf, w_gate_s, w_up_s, w_down_s)
    return (routed + shared).reshape(b, L, D)


def setup_inputs(seed: int = 0) -> dict:
    key = jax.random.key(seed)
    ks = jax.random.split(key, 32)
    f32 = jnp.float32

    def nrm(k, shape, s):
        return jax.random.normal(k, shape, f32) * s

    n_pages = PAST_LEN // PAGE_SIZE
    n_phys = (5 * DEC_BATCH * n_pages + 3) // 4
    x_prompt = nrm(ks[0], (BATCH, SEQ, D_MODEL), 1.0)
    x_sample = nrm(ks[1], (DEC_BATCH, DEC_SEQ, D_MODEL), 1.0)
    cache_k = nrm(ks[2], (DEPTH, n_phys, PAGE_SIZE, N_ATTN_HEADS, ATTN_HEAD_DIM), 1.0)
    cache_v = nrm(ks[3], (DEPTH, n_phys, PAGE_SIZE, N_ATTN_HEADS, ATTN_HEAD_DIM), 1.0)
    state_conv = nrm(ks[4], (DEPTH, DEC_BATCH, CONV_WIDTH - 1, CONV_DIM), 1.0)
    state_ssm = nrm(ks[5], (DEPTH, DEC_BATCH, N_SSM_HEADS, SSM_HEAD_DIM, SSM_STATE), 0.1)
    page_table = jax.random.permutation(ks[6], n_phys)[:DEC_BATCH * n_pages].reshape(
        DEC_BATCH, n_pages).astype(jnp.int32)
    v_lo = 2 * ATTN_WIDTH
    w_in = nrm(ks[7], (DEPTH, D_MODEL, IN_DIM), D_MODEL ** -0.5)
    w_in = w_in.at[:, :, v_lo:v_lo + ATTN_WIDTH].multiply(BETA)
    sb_bias = SB_BIAS_INIT + nrm(ks[28], (DEPTH, N_ATTN_HEADS), 0.5)
    conv_w = nrm(ks[8], (DEPTH, CONV_WIDTH, CONV_DIM), CONV_WIDTH ** -0.5)
    conv_b = nrm(ks[9], (DEPTH, CONV_DIM), 0.02)
    dt0 = jnp.exp(jax.random.uniform(ks[10], (DEPTH, N_SSM_HEADS), f32)
                  * (math.log(0.1) - math.log(0.001)) + math.log(0.001))
    dt_bias = dt0 + jnp.log(-jnp.expm1(-dt0))
    a_log = jnp.log(jax.random.uniform(ks[11], (DEPTH, N_SSM_HEADS), f32, minval=1.0, maxval=16.0))
    d_skip = 1.0 + nrm(ks[12], (DEPTH, N_SSM_HEADS), 0.1)
    g_attn = 1.0 + nrm(ks[13], (DEPTH, ATTN_WIDTH), 0.1)
    g_ssd = 1.0 + nrm(ks[14], (DEPTH, SSM_WIDTH), 0.1)
    w_out = nrm(ks[15], (DEPTH, D_MIX, D_MODEL), D_MIX ** -0.5 * BETA)
    ln1_g = 1.0 + nrm(ks[16], (DEPTH, D_MODEL), 0.1)
    ln1_b = nrm(ks[17], (DEPTH, D_MODEL), 0.02)
    w_router = nrm(ks[18], (DEPTH, D_MODEL, N_EXPERTS), D_MODEL ** -0.5)
    router_bias = nrm(ks[19], (DEPTH, N_EXPERTS), 0.01)
    w_gate_e = nrm(ks[20], (DEPTH, N_EXPERTS, D_MODEL, D_EXPERT), D_MODEL ** -0.5)
    w_up_e = nrm(ks[21], (DEPTH, N_EXPERTS, D_MODEL, D_EXPERT), D_MODEL ** -0.5)
    w_down_e = nrm(ks[22], (DEPTH, N_EXPERTS, D_EXPERT, D_MODEL), D_EXPERT ** -0.5 * BETA)
    w_gate_s = nrm(ks[23], (DEPTH, D_MODEL, D_SHARED), D_MODEL ** -0.5)
    w_up_s = nrm(ks[24], (DEPTH, D_MODEL, D_SHARED), D_MODEL ** -0.5)
    w_down_s = nrm(ks[25], (DEPTH, D_SHARED, D_MODEL), D_SHARED ** -0.5 * BETA)
    ln2_g = 1.0 + nrm(ks[26], (DEPTH, D_MODEL), 0.1)
    ln2_b = nrm(ks[27], (DEPTH, D_MODEL), 0.02)
    return {'x_prompt': x_prompt, 'x_sample': x_sample, 'cache_k': cache_k, 'cache_v': cache_v,
            'state_conv': state_conv, 'state_ssm': state_ssm, 'page_table': page_table,
            'w_in': w_in, 'sb_bias': sb_bias, 'conv_w': conv_w, 'conv_b': conv_b, 'dt_bias': dt_bias,
            'a_log': a_log, 'd_skip': d_skip, 'g_attn': g_attn, 'g_ssd': g_ssd, 'w_out': w_out,
            'ln1_g': ln1_g, 'ln1_b': ln1_b, 'w_router': w_router, 'router_bias': router_bias,
            'w_gate_e': w_gate_e, 'w_up_e': w_up_e, 'w_down_e': w_down_e,
            'w_gate_s': w_gate_s, 'w_up_s': w_up_s, 'w_down_s': w_down_s,
            'ln2_g': ln2_g, 'ln2_b': ln2_b}


def reference(x_prompt, x_sample, cache_k, cache_v, state_conv, state_ssm, page_table,
              w_in, sb_bias, conv_w, conv_b, dt_bias, a_log, d_skip, g_attn, g_ssd, w_out,
              ln1_g, ln1_b, w_router, router_bias, w_gate_e, w_up_e, w_down_e,
              w_gate_s, w_up_s, w_down_s, ln2_g, ln2_b):
    yp, ys = x_prompt, x_sample
    bp, bs = yp.shape[0], ys.shape[0]
    kp_l, vp_l, cp_l, hp_l = [], [], [], []
    ks_l, vs_l, cs_l, hs_l = [], [], [], []
    for l in range(DEPTH):
        mix_w = (w_in[l], sb_bias[l], conv_w[l], conv_b[l], dt_bias[l], a_log[l], d_skip[l],
                 g_attn[l], g_ssd[l], w_out[l])
        moe_w = (w_router[l], router_bias[l], w_gate_e[l], w_up_e[l], w_down_e[l],
                 w_gate_s[l], w_up_s[l], w_down_s[l])
        conv0 = jnp.zeros((bp, CONV_WIDTH - 1, CONV_DIM), yp.dtype)
        h0 = jnp.zeros((bp, N_SSM_HEADS, SSM_HEAD_DIM, SSM_STATE), jnp.float32)
        m, k_new, v_new, c_new, h_new = mixer(yp, sb_attend_prompt, conv0, h0, *mix_w)
        yp = layer_norm(ALPHA * yp + m, ln1_g[l], ln1_b[l])
        yp = layer_norm(ALPHA * yp + moe(yp, *moe_w), ln2_g[l], ln2_b[l])
        kp_l.append(k_new); vp_l.append(v_new); cp_l.append(c_new); hp_l.append(h_new)
        k_past = cache_k[l][page_table].reshape(bs, -1, N_ATTN_HEADS, ATTN_HEAD_DIM)
        v_past = cache_v[l][page_table].reshape(bs, -1, N_ATTN_HEADS, ATTN_HEAD_DIM)
        attend = functools.partial(sb_attend_sample, k_past=k_past, v_past=v_past)
        m, k_new, v_new, c_new, h_new = mixer(ys, attend, state_conv[l], state_ssm[l], *mix_w)
        ys = layer_norm(ALPHA * ys + m, ln1_g[l], ln1_b[l])
        ys = layer_norm(ALPHA * ys + moe(ys, *moe_w), ln2_g[l], ln2_b[l])
        ks_l.append(k_new); vs_l.append(v_new); cs_l.append(c_new); hs_l.append(h_new)
    k_prompt, v_prompt = jnp.stack(kp_l), jnp.stack(vp_l)
    conv_prompt, ssm_prompt = jnp.stack(cp_l), jnp.stack(hp_l)
    k_sample, v_sample = jnp.stack(ks_l), jnp.stack(vs_l)
    conv_sample, ssm_sample = jnp.stack(cs_l), jnp.stack(hs_l)
    return (yp, ys, k_prompt, v_prompt, conv_prompt, ssm_prompt, k_sample, v_sample, conv_sample, ssm_sample)
```

```python
import functools

import jax
import jax.numpy as jnp
from jax import lax
from jax.experimental import pallas as pl
from jax.experimental.pallas import tpu as pltpu

F32 = jnp.float32
BF16 = jnp.bfloat16

D_MODEL = 2048
ATTN_WIDTH = 1024
SSM_WIDTH = 1024
HEAD_DIM = 128
N_HEADS = 8
SB_SCALE = HEAD_DIM ** -0.5
SSM_HEAD_DIM = 64
N_SSM_HEADS = 16
SSM_STATE = 128
N_GROUPS = 4
HEADS_PER_GROUP = N_SSM_HEADS // N_GROUPS
CONV_WIDTH = 4
CONV_DIM = SSM_WIDTH + 2 * N_GROUPS * SSM_STATE
IN_MAIN = 3 * ATTN_WIDTH + SSM_WIDTH + CONV_DIM
IN_PAD = IN_MAIN + 128
N_EXPERTS = 64
N_EXPERT_GROUPS = 8
TOPK_GROUPS = 4
TOP_K = 8
D_EXPERT = 512
ROUTED_SCALE = 2.5
ALPHA = 2.0 ** 0.25
LN_EPS = 1e-5
PAGE_SIZE = 128

LANES = 128
SUBLANES = 8
VMEM_LIMIT = 56 * 1024 * 1024

SSD_CHUNK = 128
ATTN_BLOCK = 256
PAGES_PER_STEP = 4
MOE_ROWS = 256
NEG_BIG = -1e30


def _params(*sem):
    return pltpu.CompilerParams(dimension_semantics=sem, vmem_limit_bytes=VMEM_LIMIT)


def _softplus(x):
    return jnp.maximum(x, 0.0) + jnp.log1p(jnp.exp(-jnp.abs(x)))


def _silu(x):
    return x * (1.0 / (1.0 + jnp.exp(-x)))


def _dot(a, b, **kw):
    return jnp.dot(a, b, preferred_element_type=F32, **kw)


def _dot_nt(a, b):
    return lax.dot_general(a, b, (((1,), (1,)), ((), ())), preferred_element_type=F32)


def _dot_tn(a, b):
    return lax.dot_general(a, b, (((0,), (0,)), ((), ())), preferred_element_type=F32)


def _proj_kernel(x_ref, w_ref, o_ref):
    o_ref[...] = _dot(x_ref[...].astype(BF16), w_ref[...])


def _proj(x, w, tm, tn):
    m, d = x.shape
    n = w.shape[1]
    return pl.pallas_call(
        _proj_kernel,
        out_shape=jax.ShapeDtypeStruct((m, n), F32),
        grid=(m // tm, n // tn),
        in_specs=[pl.BlockSpec((tm, d), lambda i, j: (i, 0)),
                  pl.BlockSpec((d, tn), lambda i, j: (0, j))],
        out_specs=pl.BlockSpec((tm, tn), lambda i, j: (i, j)),
        compiler_params=_params("parallel", "arbitrary"),
        name="in_proj",
    )(x, w)


def _suffix_ones(n):
    j = lax.broadcasted_iota(jnp.int32, (n, n), 0)
    s = lax.broadcasted_iota(jnp.int32, (n, n), 1)
    return jnp.concatenate([(j > s).astype(BF16), jnp.ones((n, n), BF16)], axis=1)


def _sb_block(z, valid, c, uo, nk):
    sp = _softplus(z)
    log_keep = -sp
    if valid is not None:
        log_keep = jnp.where(valid, log_keep, 0.0)
    lc = _dot(log_keep.astype(BF16), uo)
    a = jnp.exp((z - sp) + lc[:, :nk] + c)
    if valid is not None:
        a = jnp.where(valid, a, 0.0)
    return a, c + lc[:, nk:]


def _attn_prompt_kernel(bias_ref, q_ref, k_ref, v_ref, uo_ref, o_ref, *, blk):
    h = pl.program_id(1)
    i = pl.program_id(2)
    bias = bias_ref[h]
    q = q_ref[...].astype(BF16)
    uo = uo_ref[...]
    row = lax.broadcasted_iota(jnp.int32, (blk, blk), 0)
    col = lax.broadcasted_iota(jnp.int32, (blk, blk), 1)

    def block(j, c, acc, valid):
        start = pl.multiple_of(j * blk, blk)
        kb = k_ref[pl.ds(start, blk), :].astype(BF16)
        vb = v_ref[pl.ds(start, blk), :].astype(BF16)
        z = _dot_nt(q, kb) * SB_SCALE + bias
        a, c = _sb_block(z, valid, c, uo, blk)
        return c, acc + _dot(a.astype(BF16), vb)

    c, acc = block(i, jnp.zeros((blk, blk), F32), jnp.zeros((blk, HEAD_DIM), F32), col < row)
    c, acc = lax.fori_loop(0, i, lambda jj, carry: block(i - 1 - jj, carry[0], carry[1], None), (c, acc))
    o_ref[...] = acc


def _attn_prompt(proj, sb_bias, batch, seq):
    blk = ATTN_BLOCK
    nq = seq // blk
    return pl.pallas_call(
        functools.partial(_attn_prompt_kernel, blk=blk),
        out_shape=jax.ShapeDtypeStruct((batch * seq, ATTN_WIDTH), F32),
        grid=(batch, N_HEADS, nq),
        in_specs=[pl.BlockSpec(memory_space=pltpu.SMEM),
                  pl.BlockSpec((blk, HEAD_DIM), lambda b, h, i: (b * nq + i, h)),
                  pl.BlockSpec((seq, HEAD_DIM), lambda b, h, i: (b, N_HEADS + h)),
                  pl.BlockSpec((seq, HEAD_DIM), lambda b, h, i: (b, 2 * N_HEADS + h)),
                  pl.BlockSpec((blk, 2 * blk), lambda b, h, i: (0, 0))],
        out_specs=pl.BlockSpec((blk, HEAD_DIM), lambda b, h, i: (b * nq + i, h)),
        compiler_params=_params("parallel", "parallel", "arbitrary"),
        name="sb_attn_prompt",
    )(sb_bias, proj, proj, proj, _suffix_ones(blk))


def _attn_sample_kernel(pt_ref, bias_ref, q_ref, kn_ref, vn_ref, *rest, pps):
    kp_refs, vp_refs = rest[:pps], rest[pps:2 * pps]
    uo_ref, o_ref, wq_ref, bias_v, c_ref, acc_ref = rest[2 * pps:]
    s = pl.program_id(1)
    rows = N_HEADS * SUBLANES
    uo = uo_ref[...]

    def sweep(kp, vp, valid):
        z = _dot_nt(wq_ref[...], kp) * SB_SCALE + bias_v[...]
        a, c = _sb_block(z, valid, c_ref[...], uo, PAGE_SIZE)
        c_ref[...] = c
        acc_ref[...] += _dot(a.astype(BF16), vp)

    @pl.when(s == 0)
    def _():
        q8 = q_ref[...]
        r = lax.broadcasted_iota(jnp.int32, (rows, ATTN_WIDTH), 0)
        d = lax.broadcasted_iota(jnp.int32, (rows, ATTN_WIDTH), 1)
        wq = jnp.where(d // HEAD_DIM == r // SUBLANES, jnp.concatenate([q8] * N_HEADS, axis=0), 0.0)
        wq_ref[...] = wq.astype(BF16)
        rh = lax.broadcasted_iota(jnp.int32, (rows, PAGE_SIZE), 0) // SUBLANES
        bv = jnp.zeros((rows, PAGE_SIZE), F32)
        for h in range(N_HEADS):
            bv = jnp.where(rh == h, bias_ref[h], bv)
        bias_v[...] = bv
        c_ref[...] = jnp.zeros_like(c_ref)
        acc_ref[...] = jnp.zeros_like(acc_ref)
        pad = jnp.zeros((PAGE_SIZE - SUBLANES, ATTN_WIDTH), F32)
        kn = jnp.concatenate([kn_ref[...], pad], axis=0).astype(BF16)
        vn = jnp.concatenate([vn_ref[...], pad], axis=0).astype(BF16)
        t = lax.broadcasted_iota(jnp.int32, (rows, PAGE_SIZE), 0) % SUBLANES
        key = lax.broadcasted_iota(jnp.int32, (rows, PAGE_SIZE), 1)
        sweep(kn, vn, key < t)

    for u in range(pps):
        sweep(kp_refs[u][...].astype(BF16), vp_refs[u][...].astype(BF16), None)

    @pl.when(s == pl.num_programs(1) - 1)
    def _():
        acc = acc_ref[...]
        o_ref[...] = jnp.concatenate(
            [acc[h * SUBLANES:(h + 1) * SUBLANES, h * HEAD_DIM:(h + 1) * HEAD_DIM] for h in range(N_HEADS)],
            axis=1)


def _attn_sample(proj8, cache_k, cache_v, page_table, sb_bias):
    batch = proj8.shape[0]
    n_pages = page_table.shape[1]
    pps = PAGES_PER_STEP
    steps = n_pages // pps
    rows = N_HEADS * SUBLANES

    def tok_spec(col):
        return pl.BlockSpec((None, SUBLANES, ATTN_WIDTH), lambda b, s, pt: (b, 0, col))

    def page_spec(u):
        return pl.BlockSpec((None, PAGE_SIZE, ATTN_WIDTH),
                            lambda b, s, pt: (pt[b, n_pages - 1 - (s * pps + u)], 0, 0))

    grid_spec = pltpu.PrefetchScalarGridSpec(
        num_scalar_prefetch=1,
        grid=(batch, steps),
        in_specs=[pl.BlockSpec(memory_space=pltpu.SMEM), tok_spec(0), tok_spec(1), tok_spec(2)]
        + [page_spec(u) for u in range(pps)] * 2
        + [pl.BlockSpec((PAGE_SIZE, 2 * PAGE_SIZE), lambda b, s, pt: (0, 0))],
        out_specs=pl.BlockSpec((None, SUBLANES, ATTN_WIDTH), lambda b, s, pt: (b, 0, 0)),
        scratch_shapes=[pltpu.VMEM((rows, ATTN_WIDTH), BF16),
                        pltpu.VMEM((rows, PAGE_SIZE), F32),
                        pltpu.VMEM((rows, PAGE_SIZE), F32),
                        pltpu.VMEM((rows, ATTN_WIDTH), F32)])
    return pl.pallas_call(
        functools.partial(_attn_sample_kernel, pps=pps),
        out_shape=jax.ShapeDtypeStruct((batch, SUBLANES, ATTN_WIDTH), F32),
        grid_spec=grid_spec,
        compiler_params=_params("parallel", "arbitrary"),
        name="sb_attn_sample",
    )(page_table, sb_bias, proj8, proj8, proj8,
      *([cache_k] * pps), *([cache_v] * pps), _suffix_ones(PAGE_SIZE))


def _ssd_kernel(dskip_ref, x_ref, conv0_ref, h0_ref, dtc_ref, dtr_ref, cw_ref, cb_ref,
                dtb_c_ref, alog_c_ref, dtb_r_ref, alog_r_ref, y_ref, hout_ref, ext_ref,
                *, rows_in, valid):
    q = SSD_CHUNK
    c = pl.program_id(1)

    @pl.when(c == 0)
    def _():
        ext_ref[0:SUBLANES, :] = conv0_ref[...]
        hout_ref[...] = h0_ref[...]

    if rows_in < q:
        ext_ref[SUBLANES:, :] = jnp.zeros((q, CONV_DIM), F32)
    ext_ref[SUBLANES:SUBLANES + rows_in, :] = x_ref[...]
    cw = cw_ref[...]
    xc = cb_ref[...] + ext_ref[5:5 + q, :] * cw[0:1, :]
    xc = xc + ext_ref[6:6 + q, :] * cw[1:2, :]
    xc = xc + ext_ref[7:7 + q, :] * cw[2:3, :]
    xc = _silu(xc + ext_ref[8:8 + q, :] * cw[3:4, :])
    ext_ref[0:SUBLANES, :] = ext_ref[q:q + SUBLANES, :]

    t_c = lax.broadcasted_iota(jnp.int32, (q, LANES), 0)
    h_c = lax.broadcasted_iota(jnp.int32, (q, LANES), 1)
    dt_c = jnp.where((t_c < valid) & (h_c < N_SSM_HEADS), _softplus(dtc_ref[...] + dtb_c_ref[...]), 0.0)
    t_r = lax.broadcasted_iota(jnp.int32, (N_SSM_HEADS, q), 1)
    dt_r = jnp.where(t_r < valid, _softplus(dtr_ref[...] + dtb_r_ref[...]), 0.0)
    ti = lax.broadcasted_iota(jnp.int32, (q, q), 0)
    tj = lax.broadcasted_iota(jnp.int32, (q, q), 1)
    causal = tj <= ti
    acum_c = _dot(causal.astype(F32), dt_c * -jnp.exp(alog_c_ref[...]), precision=lax.Precision.HIGHEST)
    acum_r = _dot(dt_r * -jnp.exp(alog_r_ref[...]), (ti <= tj).astype(F32), precision=lax.Precision.HIGHEST)

    for g in range(N_GROUPS):
        lo = SSM_WIDTH + g * SSM_STATE
        bg = xc[:, lo:lo + SSM_STATE].astype(BF16)
        lo = SSM_WIDTH + (N_GROUPS + g) * SSM_STATE
        cg = xc[:, lo:lo + SSM_STATE].astype(BF16)
        cb = _dot_nt(cg, bg)
        for hh in range(HEADS_PER_GROUP):
            h = g * HEADS_PER_GROUP + hh
            xs = xc[:, h * SSM_HEAD_DIM:(h + 1) * SSM_HEAD_DIM]
            ac = acum_c[:, h:h + 1]
            dth = dt_c[:, h:h + 1]
            a_last = acum_r[h:h + 1, q - 1:q]
            decay = jnp.exp(jnp.where(causal, ac - acum_r[h:h + 1, :], NEG_BIG))
            y = _dot((cb * decay).astype(BF16), (xs * dth).astype(BF16))
            hst = hout_ref[h]
            y = y + _dot_nt(cg, hst.astype(BF16)) * jnp.exp(ac)
            y = y + dskip_ref[h] * xs
            xw = (xs * (jnp.exp(a_last - ac) * dth)).astype(BF16)
            hout_ref[h] = hst * jnp.exp(a_last) + _dot_tn(xw, bg)
            y_ref[:, h * SSM_HEAD_DIM:(h + 1) * SSM_HEAD_DIM] = y[:rows_in]


def _ssd(x3, col_blk, conv0, h0, dtc, dtc_blk, dtr, conv_w, conv_b, dt_bias, a_log, d_skip, rows_in, valid):
    batch, length = x3.shape[0], x3.shape[1]
    nc = max(length // SSD_CHUNK, 1)
    lane_pad = (0, LANES - N_SSM_HEADS)
    dtb_c = jnp.pad(dt_bias, lane_pad)[None, :]
    alog_c = jnp.pad(a_log, lane_pad)[None, :]
    dtb_r = jnp.broadcast_to(dt_bias[:, None], (N_SSM_HEADS, LANES))
    alog_r = jnp.broadcast_to(a_log[:, None], (N_SSM_HEADS, LANES))
    const2 = lambda b, c: (0, 0)
    return pl.pallas_call(
        functools.partial(_ssd_kernel, rows_in=rows_in, valid=valid),
        out_shape=(jax.ShapeDtypeStruct((batch, length, SSM_WIDTH), F32),
                   jax.ShapeDtypeStruct((batch, N_SSM_HEADS, SSM_HEAD_DIM, SSM_STATE), F32)),
        grid=(batch, nc),
        in_specs=[pl.BlockSpec(memory_space=pltpu.SMEM),
                  pl.BlockSpec((None, rows_in, CONV_DIM), lambda b, c: (b, c, col_blk)),
                  pl.BlockSpec((None, SUBLANES, CONV_DIM), lambda b, c: (b, 0, 0)),
                  pl.BlockSpec((None, N_SSM_HEADS, SSM_HEAD_DIM, SSM_STATE), lambda b, c: (b, 0, 0, 0)),
                  pl.BlockSpec((None, SSD_CHUNK, LANES), lambda b, c: (b, c, dtc_blk)),
                  pl.BlockSpec((None, N_SSM_HEADS, SSD_CHUNK), lambda b, c: (b, 0, c)),
                  pl.BlockSpec((CONV_WIDTH, CONV_DIM), const2),
                  pl.BlockSpec((1, CONV_DIM), const2),
                  pl.BlockSpec((1, LANES), const2),
                  pl.BlockSpec((1, LANES), const2),
                  pl.BlockSpec((N_SSM_HEADS, LANES), const2),
                  pl.BlockSpec((N_SSM_HEADS, LANES), const2)],
        out_specs=(pl.BlockSpec((None, rows_in, SSM_WIDTH), lambda b, c: (b, c, 0)),
                   pl.BlockSpec((None, N_SSM_HEADS, SSM_HEAD_DIM, SSM_STATE), lambda b, c: (b, 0, 0, 0))),
        scratch_shapes=[pltpu.VMEM((SSD_CHUNK + SUBLANES, CONV_DIM), F32)],
        compiler_params=_params("parallel", "arbitrary"),
        name="ssd_mixer",
    )(d_skip, x3, conv0, h0, dtc, dtr, conv_w, conv_b[None, :], dtb_c, alog_c, dtb_r, alog_r)


def _layer_norm(u, g, b):
    mu = jnp.mean(u, axis=-1, keepdims=True)
    d = u - mu
    var = jnp.mean(d * d, axis=-1, keepdims=True)
    return d * lax.rsqrt(var + LN_EPS) * g + b


def _mix_out_kernel(oa_ref, ys_ref, z_ref, x_ref, ga_ref, gs_ref, wo_ref, g1_ref, b1_ref, wr_ref,
                    x1_ref, x1b_ref, lg_ref):
    oa = oa_ref[...]
    na = oa * lax.rsqrt(jnp.mean(oa * oa, axis=-1, keepdims=True) + LN_EPS) * ga_ref[...]
    m = _dot(na.astype(BF16), wo_ref[0:ATTN_WIDTH, :])
    yz = ys_ref[...] * _silu(z_ref[...])
    gw = SSM_WIDTH // N_GROUPS
    gs = gs_ref[...]
    for g in range(N_GROUPS):
        yg = yz[:, g * gw:(g + 1) * gw]
        ng = yg * lax.rsqrt(jnp.mean(yg * yg, axis=-1, keepdims=True) + LN_EPS) * gs[:, g * gw:(g + 1) * gw]
        lo = ATTN_WIDTH + g * gw
        m = m + _dot(ng.astype(BF16), wo_ref[lo:lo + gw, :])
    x1 = _layer_norm(ALPHA * x_ref[...] + m, g1_ref[...], b1_ref[...])
    x1_ref[...] = x1
    x1b = x1.astype(BF16)
    x1b_ref[...] = x1b
    lg_ref[...] = _dot(x1b, wr_ref[...])


def _mix_out(o_attn, y_ssd, proj, x, g_attn, g_ssd, w_out, ln_g, ln_b, w_router, tm):
    m = x.shape[0]
    row = lambda i: (i, 0)
    const = lambda i: (0, 0)
    return pl.pallas_call(
        _mix_out_kernel,
        out_shape=(jax.ShapeDtypeStruct((m, D_MODEL), F32),
                   jax.ShapeDtypeStruct((m, D_MODEL), BF16),
                   jax.ShapeDtypeStruct((m, LANES), F32)),
        grid=(m // tm,),
        in_specs=[pl.BlockSpec((tm, ATTN_WIDTH), row),
                  pl.BlockSpec((tm, SSM_WIDTH), row),
                  pl.BlockSpec((tm, SSM_WIDTH), lambda i: (i, 3)),
                  pl.BlockSpec((tm, D_MODEL), row),
                  pl.BlockSpec((1, ATTN_WIDTH), const),
                  pl.BlockSpec((1, SSM_WIDTH), const),
                  pl.BlockSpec((D_MODEL, D_MODEL), const),
                  pl.BlockSpec((1, D_MODEL), const),
                  pl.BlockSpec((1, D_MODEL), const),
                  pl.BlockSpec((D_MODEL, LANES), const)],
        out_specs=(pl.BlockSpec((tm, D_MODEL), row),
                   pl.BlockSpec((tm, D_MODEL), row),
                   pl.BlockSpec((tm, LANES), row)),
        compiler_params=_params("parallel"),
        name="mix_out_ln",
    )(o_attn, y_ssd, proj, x, g_attn[None, :], g_ssd[None, :], w_out, ln_g[None, :], ln_b[None, :], w_router)


def _moe_kernel(blk_e_ref, nvalid_ref, x_ref, gate_ref, wg_ref, wu_ref, wd_ref, o_ref, wgb, wub, wdb):
    i = pl.program_id(0)
    e = blk_e_ref[i]
    prev = blk_e_ref[jnp.maximum(i - 1, 0)]

    @pl.when((i == 0) | (e != prev))
    def _():
        wgb[...] = wg_ref[...].astype(BF16)
        wub[...] = wu_ref[...].astype(BF16)
        wdb[...] = wd_ref[...].astype(BF16)

    @pl.when(nvalid_ref[i] > 0)
    def _():
        x = x_ref[...]
        act = (_silu(_dot(x, wgb[...])) * _dot(x, wub[...])).astype(BF16)
        o_ref[...] = (_dot(act, wdb[...]) * gate_ref[...]).astype(BF16)

    @pl.when(nvalid_ref[i] == 0)
    def _():
        o_ref[...] = jnp.zeros_like(o_ref)


def _moe_experts(blk_e, nvalid, xg, gate, w_gate_e, w_up_e, w_down_e):
    cap = xg.shape[0]
    r = MOE_ROWS
    grid_spec = pltpu.PrefetchScalarGridSpec(
        num_scalar_prefetch=2,
        grid=(cap // r,),
        in_specs=[pl.BlockSpec((r, D_MODEL), lambda i, be, nv: (i, 0)),
                  pl.BlockSpec((r, 1), lambda i, be, nv: (i, 0)),
                  pl.BlockSpec((None, D_MODEL, D_EXPERT), lambda i, be, nv: (be[i], 0, 0)),
                  pl.BlockSpec((None, D_MODEL, D_EXPERT), lambda i, be, nv: (be[i], 0, 0)),
                  pl.BlockSpec((None, D_EXPERT, D_MODEL), lambda i, be, nv: (be[i], 0, 0))],
        out_specs=pl.BlockSpec((r, D_MODEL), lambda i, be, nv: (i, 0)),
        scratch_shapes=[pltpu.VMEM((D_MODEL, D_EXPERT), BF16),
                        pltpu.VMEM((D_MODEL, D_EXPERT), BF16),
                        pltpu.VMEM((D_EXPERT, D_MODEL), BF16)])
    return pl.pallas_call(
        _moe_kernel,
        out_shape=jax.ShapeDtypeStruct((cap, D_MODEL), BF16),
        grid_spec=grid_spec,
        compiler_params=_params("arbitrary"),
        name="moe_experts",
    )(blk_e, nvalid, xg, gate, w_gate_e, w_up_e, w_down_e)


def _route(logits, router_bias):
    m = logits.shape[0]
    per = N_EXPERTS // N_EXPERT_GROUPS
    scores = jax.nn.sigmoid(logits)
    choice = scores + router_bias
    grp_score = lax.top_k(choice.reshape(m, N_EXPERT_GROUPS, per), 2)[0].sum(-1)
    top_g = lax.top_k(grp_score, TOPK_GROUPS)[1]
    gmask = (top_g[..., None] == jnp.arange(N_EXPERT_GROUPS)).any(axis=-2)
    emask = jnp.repeat(gmask, per, axis=-1)
    idx = lax.top_k(jnp.where(emask, choice, -jnp.inf), TOP_K)[1]
    w = jnp.take_along_axis(scores, idx, axis=-1)
    return idx, w / jnp.sum(w, axis=-1, keepdims=True) * ROUTED_SCALE


def _dispatch(idx, gates):
    m = idx.shape[0]
    r = MOE_ROWS
    mk = m * TOP_K
    flat_e = idx.reshape(-1)
    order = jnp.argsort(flat_e)
    se = flat_e[order]
    counts = jnp.bincount(flat_e, length=N_EXPERTS)
    start = jnp.cumsum(counts) - counts
    padded = (counts + r - 1) // r * r
    pend = jnp.cumsum(padded)
    pstart = pend - padded
    dest = (pstart[se] + (jnp.arange(mk) - start[se])).astype(jnp.int32)
    nb = -(-mk // r) + N_EXPERTS
    cap = nb * r
    tok_buf = jnp.full((cap,), m, jnp.int32).at[dest].set((order // TOP_K).astype(jnp.int32))
    gate_buf = jnp.zeros((cap,), F32).at[dest].set(gates.reshape(-1)[order])
    pos = jnp.zeros((mk,), jnp.int32).at[order].set(dest)
    blk_lo = jnp.arange(nb) * r
    blk_e = jnp.minimum(jnp.searchsorted(pend, blk_lo, side='right'), N_EXPERTS - 1).astype(jnp.int32)
    nvalid = jnp.clip((pstart + counts)[blk_e] - blk_lo, 0, r).astype(jnp.int32)
    return tok_buf, gate_buf, pos.reshape(m, TOP_K), blk_e, nvalid


def _final_kernel(x1_ref, x1b_ref, routed_ref, wg_ref, wu_ref, wd_ref, g_ref, b_ref, o_ref):
    xb = x1b_ref[...]
    act = (_silu(_dot(xb, wg_ref[...])) * _dot(xb, wu_ref[...])).astype(BF16)
    moe = routed_ref[...] + _dot(act, wd_ref[...])
    o_ref[...] = _layer_norm(ALPHA * x1_ref[...] + moe, g_ref[...], b_ref[...])


def _final(x1, x1b, routed, row_off, w_gate_s, w_up_s, w_down_s, ln_g, ln_b, tm):
    m = x1.shape[0]
    off = row_off // tm
    row = lambda i: (i, 0)
    const = lambda i: (0, 0)
    return pl.pallas_call(
        _final_kernel,
        out_shape=jax.ShapeDtypeStruct((m, D_MODEL), F32),
        grid=(m // tm,),
        in_specs=[pl.BlockSpec((tm, D_MODEL), row),
                  pl.BlockSpec((tm, D_MODEL), row),
                  pl.BlockSpec((tm, D_MODEL), lambda i: (i + off, 0)),
                  pl.BlockSpec((D_MODEL, D_EXPERT), const),
                  pl.BlockSpec((D_MODEL, D_EXPERT), const),
                  pl.BlockSpec((D_EXPERT, D_MODEL), const),
                  pl.BlockSpec((1, D_MODEL), const),
                  pl.BlockSpec((1, D_MODEL), const)],
        out_specs=pl.BlockSpec((tm, D_MODEL), row),
        compiler_params=_params("parallel"),
        name="shared_ln",
    )(x1, x1b, routed, w_gate_s, w_up_s, w_down_s, ln_g[None, :], ln_b[None, :])


def kernel(x_prompt, x_sample, cache_k, cache_v, state_conv, state_ssm, page_table, w_in, sb_bias, conv_w, conv_b, dt_bias, a_log, d_skip, g_attn, g_ssd, w_out, ln1_g, ln1_b, w_router, router_bias, w_gate_e, w_up_e, w_down_e, w_gate_s, w_up_s, w_down_s, ln2_g, ln2_b):
    assert w_in.shape[0] == 1, "one layer"
    bp, seq, _ = x_prompt.shape
    bs, dec, _ = x_sample.shape
    mp, ms = bp * seq, bs * dec
    n_phys = cache_k.shape[1]

    w_in_b = jnp.pad(w_in[0], ((0, 0), (0, IN_PAD - w_in.shape[2]))).astype(BF16)
    w_out_b = w_out[0].astype(BF16)
    w_router_b = jnp.pad(w_router[0], ((0, 0), (0, LANES - N_EXPERTS))).astype(BF16)
    mix_w = (g_attn[0], g_ssd[0], w_out_b, ln1_g[0], ln1_b[0], w_router_b)
    ssd_w = (conv_w[0], conv_b[0], dt_bias[0], a_log[0], d_skip[0])
    dt_blk = IN_MAIN // LANES
    xbc_blk = (3 * ATTN_WIDTH + SSM_WIDTH) // CONV_DIM

    xp = x_prompt.reshape(mp, D_MODEL)
    proj_p = _proj(xp, w_in_b, 512, 896)
    proj_p3 = proj_p.reshape(bp, seq, IN_PAD)
    k_prompt = proj_p3[:, :, ATTN_WIDTH:2 * ATTN_WIDTH].reshape(1, bp, seq, N_HEADS, HEAD_DIM)
    v_prompt = proj_p3[:, :, 2 * ATTN_WIDTH:3 * ATTN_WIDTH].reshape(1, bp, seq, N_HEADS, HEAD_DIM)
    conv_prompt = proj_p3[:, seq - (CONV_WIDTH - 1):, IN_MAIN - CONV_DIM:IN_MAIN][None]
    oa_p = _attn_prompt(proj_p, sb_bias[0], bp, seq)
    dtr_p = proj_p3[:, :, IN_MAIN:IN_MAIN + N_SSM_HEADS].transpose(0, 2, 1)
    y_p, h_p = _ssd(proj_p3, xbc_blk, jnp.zeros((bp, SUBLANES, CONV_DIM), F32),
                    jnp.zeros((bp, N_SSM_HEADS, SSM_HEAD_DIM, SSM_STATE), F32),
                    proj_p3, dt_blk, dtr_p, *ssd_w, rows_in=SSD_CHUNK, valid=SSD_CHUNK)
    x1_p, x1b_p, lg_p = _mix_out(oa_p, y_p.reshape(mp, SSM_WIDTH), proj_p, xp, *mix_w, tm=256)

    xs = x_sample.reshape(ms, D_MODEL)
    proj_s = _proj(xs, w_in_b, ms, 896)
    proj_s3 = proj_s.reshape(bs, dec, IN_PAD)
    k_sample = proj_s3[:, :, ATTN_WIDTH:2 * ATTN_WIDTH].reshape(1, bs, dec, N_HEADS, HEAD_DIM)
    v_sample = proj_s3[:, :, 2 * ATTN_WIDTH:3 * ATTN_WIDTH].reshape(1, bs, dec, N_HEADS, HEAD_DIM)
    conv_sample = proj_s3[:, dec - (CONV_WIDTH - 1):, IN_MAIN - CONV_DIM:IN_MAIN][None]
    proj_s8 = jnp.pad(proj_s3, ((0, 0), (0, SUBLANES - dec), (0, 0)))
    oa_s = _attn_sample(proj_s8, cache_k.reshape(n_phys, PAGE_SIZE, ATTN_WIDTH),
                        cache_v.reshape(n_phys, PAGE_SIZE, ATTN_WIDTH), page_table, sb_bias[0])
    dtc_s = jnp.pad(proj_s8[:, :, IN_MAIN:], ((0, 0), (0, SSD_CHUNK - SUBLANES), (0, 0)))
    dtr_s = dtc_s[:, :, :N_SSM_HEADS].transpose(0, 2, 1)
    conv0_s = jnp.pad(state_conv[0], ((0, 0), (SUBLANES - (CONV_WIDTH - 1), 0), (0, 0)))
    y_s, h_s = _ssd(proj_s8, xbc_blk, conv0_s, state_ssm[0], dtc_s, 0, dtr_s, *ssd_w,
                    rows_in=SUBLANES, valid=dec)
    x1_s, x1b_s, lg_s = _mix_out(oa_s[:, :dec].reshape(ms, ATTN_WIDTH), y_s[:, :dec].reshape(ms, SSM_WIDTH),
                                 proj_s, xs, *mix_w, tm=ms)

    m = mp + ms
    logits = jnp.concatenate([lg_p, lg_s], axis=0)[:, :N_EXPERTS]
    idx, gates = _route(logits, router_bias[0])
    tok_buf, gate_buf, pos, blk_e, nvalid = _dispatch(idx, gates)
    x1b = jnp.concatenate([x1b_p, x1b_s, jnp.zeros((1, D_MODEL), BF16)], axis=0)
    out_rows = _moe_experts(blk_e, nvalid, x1b[tok_buf], gate_buf[:, None], w_gate_e[0], w_up_e[0], w_down_e[0])
    routed = out_rows[pos.reshape(-1)].reshape(m, TOP_K, D_MODEL).astype(F32).sum(axis=1)
    shared_w = (w_gate_s[0].astype(BF16), w_up_s[0].astype(BF16), w_down_s[0].astype(BF16), ln2_g[0], ln2_b[0])
    y_p = _final(x1_p, x1b_p, routed, 0, *shared_w, tm=256)
    y_s = _final(x1_s, x1b_s, routed, mp, *shared_w, tm=ms)

    return (y_p.reshape(bp, seq, D_MODEL), y_s.reshape(bs, dec, D_MODEL),
            k_prompt, v_prompt, conv_prompt, h_p[None],
            k_sample, v_sample, conv_sample, h_s[None])
```

```python
import functools

import jax
import jax.numpy as jnp
from jax import lax
from jax.experimental import pallas as pl
from jax.experimental.pallas import tpu as pltpu

F32 = jnp.float32
BF16 = jnp.bfloat16

D_MODEL = 2048
ATTN_WIDTH = 1024
SSM_WIDTH = 1024
HEAD_DIM = 128
N_HEADS = 8
SB_SCALE = HEAD_DIM ** -0.5
SSM_HEAD_DIM = 64
N_SSM_HEADS = 16
SSM_STATE = 128
N_GROUPS = 4
HEADS_PER_GROUP = N_SSM_HEADS // N_GROUPS
CONV_WIDTH = 4
CONV_DIM = SSM_WIDTH + 2 * N_GROUPS * SSM_STATE
IN_MAIN = 3 * ATTN_WIDTH + SSM_WIDTH + CONV_DIM
IN_PAD = IN_MAIN + 128
N_EXPERTS = 64
N_EXPERT_GROUPS = 8
EXPERTS_PER_GROUP = N_EXPERTS // N_EXPERT_GROUPS
TOPK_GROUPS = 4
TOP_K = 8
D_EXPERT = 512
ROUTED_SCALE = 2.5
ALPHA = 2.0 ** 0.25
LN_EPS = 1e-5
PAGE_SIZE = 128

LANES = 128
SUBLANES = 8
VMEM_LIMIT = 56 * 1024 * 1024

SSD_CHUNK = 128
ATTN_BLOCK = 256
ATTN_HEADS_PER_STEP = 4
PAGES_PER_STEP = 4
MOE_ROWS = 256
ROUTE_TILE = 128
NEG_BIG = -1e30
LOG2E = 1.4426950408889634


def _params(*sem):
    return pltpu.CompilerParams(dimension_semantics=sem, vmem_limit_bytes=VMEM_LIMIT)


def _softplus(x):
    return jnp.maximum(x, 0.0) + jnp.log1p(jnp.exp(-jnp.abs(x)))


def _silu(x):
    return x * (1.0 / (1.0 + jnp.exp(-x)))


def _dot(a, b, **kw):
    return jnp.dot(a, b, preferred_element_type=F32, **kw)


def _dot_nt(a, b):
    return lax.dot_general(a, b, (((1,), (1,)), ((), ())), preferred_element_type=F32)


def _dot_tn(a, b):
    return lax.dot_general(a, b, (((0,), (0,)), ((), ())), preferred_element_type=F32)


def _proj_kernel(x_ref, w_ref, o_ref):
    o_ref[...] = _dot(x_ref[...].astype(BF16), w_ref[...])


def _proj(x, w, tm, tn):
    m, d = x.shape
    n = w.shape[1]
    return pl.pallas_call(
        _proj_kernel,
        out_shape=jax.ShapeDtypeStruct((m, n), F32),
        grid=(m // tm, n // tn),
        in_specs=[pl.BlockSpec((tm, d), lambda i, j: (i, 0)),
                  pl.BlockSpec((d, tn), lambda i, j: (0, j))],
        out_specs=pl.BlockSpec((tm, tn), lambda i, j: (i, j)),
        compiler_params=_params("parallel", "arbitrary"),
        name="in_proj",
    )(x, w)


def _neg_suffix(n):
    j = lax.broadcasted_iota(jnp.int32, (n, n), 0)
    s = lax.broadcasted_iota(jnp.int32, (n, n), 1)
    return -jnp.concatenate([(j > s).astype(BF16), jnp.ones((n, LANES), BF16)], axis=1)


def _sb_block(z, valid, c, uo, nk):
    sp = jnp.maximum(z, 0.0) + jnp.log(1.0 + jnp.exp2(jnp.abs(z) * -LOG2E))
    spm = sp if valid is None else jnp.where(valid, sp, 0.0)
    lc = _dot(spm.astype(BF16), uo)
    cb = c if nk == LANES else jnp.concatenate([c] * (nk // LANES), axis=1)
    a = jnp.exp2(((z - sp) + lc[:, :nk] + cb) * LOG2E)
    if valid is not None:
        a = jnp.where(valid, a, 0.0)
    return a, c + lc[:, nk:]


def _attn_prompt_kernel(bias_ref, q_ref, k_ref, v_ref, uo_ref, o_ref, kb_ref, vb_ref, *, blk, heads):
    hg = pl.program_id(1)
    i = pl.program_id(2)

    @pl.when(i == 0)
    def _():
        kb_ref[...] = k_ref[...].astype(BF16)
        vb_ref[...] = v_ref[...].astype(BF16)

    uo = uo_ref[...]
    row = lax.broadcasted_iota(jnp.int32, (blk, blk), 0)
    col = lax.broadcasted_iota(jnp.int32, (blk, blk), 1)
    qs = [q_ref[:, h * HEAD_DIM:(h + 1) * HEAD_DIM].astype(BF16) for h in range(heads)]
    biases = [bias_ref[hg * heads + h] for h in range(heads)]

    def block(j, carry, valid):
        start = pl.multiple_of(j * blk, blk)
        out = []
        for h in range(heads):
            c, acc = carry[h]
            lanes = slice(h * HEAD_DIM, (h + 1) * HEAD_DIM)
            z = _dot_nt(qs[h], kb_ref[pl.ds(start, blk), lanes]) * SB_SCALE + biases[h]
            a, c = _sb_block(z, valid, c, uo, blk)
            out.append((c, acc + _dot(a.astype(BF16), vb_ref[pl.ds(start, blk), lanes])))
        return tuple(out)

    zero = jnp.zeros((blk, HEAD_DIM), F32)
    carry = block(i, ((zero, zero),) * heads, col < row)
    carry = lax.fori_loop(0, i, lambda jj, cr: block(i - 1 - jj, cr, None), carry)
    for h in range(heads):
        o_ref[:, h * HEAD_DIM:(h + 1) * HEAD_DIM] = carry[h][1]


def _attn_prompt(proj, sb_bias, batch, seq):
    blk = ATTN_BLOCK
    heads = ATTN_HEADS_PER_STEP
    nq = seq // blk
    ng = N_HEADS // heads
    width = heads * HEAD_DIM
    return pl.pallas_call(
        functools.partial(_attn_prompt_kernel, blk=blk, heads=heads),
        out_shape=jax.ShapeDtypeStruct((batch * seq, ATTN_WIDTH), F32),
        grid=(batch, ng, nq),
        in_specs=[pl.BlockSpec(memory_space=pltpu.SMEM),
                  pl.BlockSpec((blk, width), lambda b, g, i: (b * nq + i, g)),
                  pl.BlockSpec((seq, width), lambda b, g, i: (b, ng + g)),
                  pl.BlockSpec((seq, width), lambda b, g, i: (b, 2 * ng + g)),
                  pl.BlockSpec((blk, blk + LANES), lambda b, g, i: (0, 0))],
        out_specs=pl.BlockSpec((blk, width), lambda b, g, i: (b * nq + i, g)),
        scratch_shapes=[pltpu.VMEM((seq, width), BF16), pltpu.VMEM((seq, width), BF16)],
        compiler_params=_params("parallel", "parallel", "arbitrary"),
        name="sb_attn_prompt",
    )(sb_bias, proj, proj, proj, _neg_suffix(blk))


def _attn_sample_kernel(pt_ref, bias_ref, q_ref, kn_ref, vn_ref, *rest, pps):
    kp_refs, vp_refs = rest[:pps], rest[pps:2 * pps]
    uo_ref, o_ref, qb_ref, bias_v, c_ref, acc_ref = rest[2 * pps:]
    s = pl.program_id(1)
    rows = N_HEADS * SUBLANES
    uo = uo_ref[...]

    def sweep(blocks, valid):
        zs = [jnp.concatenate([_dot_nt(qb_ref[:, h * HEAD_DIM:(h + 1) * HEAD_DIM], load_k(h))
                               for h in range(N_HEADS)], axis=0) * SB_SCALE + bias_v[...]
              for load_k, _ in blocks]
        c = c_ref[...]
        acc = acc_ref[...]
        for z, (_, load_v) in zip(zs, blocks):
            a, c = _sb_block(z, valid, c, uo, PAGE_SIZE)
            acc = acc + jnp.concatenate(
                [_dot(a[h * SUBLANES:(h + 1) * SUBLANES].astype(BF16), load_v(h)) for h in range(N_HEADS)], axis=0)
        c_ref[...] = c
        acc_ref[...] = acc

    @pl.when(s == 0)
    def _():
        qb_ref[...] = q_ref[...].astype(BF16)
        rh = lax.broadcasted_iota(jnp.int32, (rows, PAGE_SIZE), 0) // SUBLANES
        bv = jnp.zeros((rows, PAGE_SIZE), F32)
        for h in range(N_HEADS):
            bv = jnp.where(rh == h, bias_ref[h], bv)
        bias_v[...] = bv
        c_ref[...] = jnp.zeros_like(c_ref)
        acc_ref[...] = jnp.zeros_like(acc_ref)
        pad = jnp.zeros((PAGE_SIZE - SUBLANES, HEAD_DIM), F32)

        def new_rows(ref):
            return lambda h: jnp.concatenate([ref[:, h * HEAD_DIM:(h + 1) * HEAD_DIM], pad], axis=0).astype(BF16)

        t = lax.broadcasted_iota(jnp.int32, (rows, PAGE_SIZE), 0) % SUBLANES
        key = lax.broadcasted_iota(jnp.int32, (rows, PAGE_SIZE), 1)
        sweep([(new_rows(kn_ref), new_rows(vn_ref))], key < t)

    def page_rows(ref):
        return lambda h: ref[pl.ds(h, PAGE_SIZE, stride=N_HEADS), :].astype(BF16)

    sweep([(page_rows(kp_refs[u]), page_rows(vp_refs[u])) for u in range(pps)], None)

    @pl.when(s == pl.num_programs(1) - 1)
    def _():
        for h in range(N_HEADS):
            o_ref[:, h * HEAD_DIM:(h + 1) * HEAD_DIM] = acc_ref[h * SUBLANES:(h + 1) * SUBLANES, :]


def _attn_sample(proj8, cache_k, cache_v, page_table, sb_bias):
    batch = proj8.shape[0]
    n_pages = page_table.shape[1]
    pps = PAGES_PER_STEP
    steps = n_pages // pps
    rows = N_HEADS * SUBLANES

    def tok_spec(col):
        return pl.BlockSpec((None, SUBLANES, ATTN_WIDTH), lambda b, s, pt: (b, 0, col))

    def page_spec(u):
        return pl.BlockSpec((None, PAGE_SIZE * N_HEADS, HEAD_DIM),
                            lambda b, s, pt: (pt[b, n_pages - 1 - (s * pps + u)], 0, 0))

    grid_spec = pltpu.PrefetchScalarGridSpec(
        num_scalar_prefetch=1,
        grid=(batch, steps),
        in_specs=[pl.BlockSpec(memory_space=pltpu.SMEM), tok_spec(0), tok_spec(1), tok_spec(2)]
        + [page_spec(u) for u in range(pps)] * 2
        + [pl.BlockSpec((PAGE_SIZE, 2 * LANES), lambda b, s, pt: (0, 0))],
        out_specs=pl.BlockSpec((None, SUBLANES, ATTN_WIDTH), lambda b, s, pt: (b, 0, 0)),
        scratch_shapes=[pltpu.VMEM((SUBLANES, ATTN_WIDTH), BF16),
                        pltpu.VMEM((rows, PAGE_SIZE), F32),
                        pltpu.VMEM((rows, LANES), F32),
                        pltpu.VMEM((rows, HEAD_DIM), F32)])
    return pl.pallas_call(
        functools.partial(_attn_sample_kernel, pps=pps),
        out_shape=jax.ShapeDtypeStruct((batch, SUBLANES, ATTN_WIDTH), F32),
        grid_spec=grid_spec,
        compiler_params=_params("parallel", "arbitrary"),
        name="sb_attn_sample",
    )(page_table, sb_bias, proj8, proj8, proj8,
      *([cache_k] * pps), *([cache_v] * pps), _neg_suffix(PAGE_SIZE))


def _ssd_kernel(x_ref, conv0_ref, h0_ref, dtc_ref, dtr_ref, cw_ref, cb_ref,
                dtb_c_ref, alog_c_ref, dtb_r_ref, alog_r_ref, dskip_ref, y_ref, hout_ref, ext_ref,
                *, rows_in, valid):
    q = SSD_CHUNK
    c = pl.program_id(1)

    @pl.when(c == 0)
    def _():
        ext_ref[0:SUBLANES, :] = conv0_ref[...]
        hout_ref[...] = h0_ref[...]

    if rows_in < q:
        ext_ref[SUBLANES:, :] = jnp.zeros((q, CONV_DIM), F32)
    ext_ref[SUBLANES:SUBLANES + rows_in, :] = x_ref[...]
    cw = cw_ref[...]
    xc = cb_ref[...] + ext_ref[5:5 + q, :] * cw[0:1, :]
    xc = xc + ext_ref[6:6 + q, :] * cw[1:2, :]
    xc = xc + ext_ref[7:7 + q, :] * cw[2:3, :]
    xc = _silu(xc + ext_ref[8:8 + q, :] * cw[3:4, :])
    ext_ref[0:SUBLANES, :] = ext_ref[q:q + SUBLANES, :]

    t_c = lax.broadcasted_iota(jnp.int32, (q, LANES), 0)
    h_c = lax.broadcasted_iota(jnp.int32, (q, LANES), 1)
    dt_c = jnp.where((t_c < valid) & (h_c < N_SSM_HEADS), _softplus(dtc_ref[...] + dtb_c_ref[...]), 0.0)
    t_r = lax.broadcasted_iota(jnp.int32, (N_SSM_HEADS, q), 1)
    dt_r = jnp.where(t_r < valid, _softplus(dtr_ref[...] + dtb_r_ref[...]), 0.0)
    ti = lax.broadcasted_iota(jnp.int32, (q, q), 0)
    tj = lax.broadcasted_iota(jnp.int32, (q, q), 1)
    causal = tj <= ti
    acum_c = _dot(causal.astype(F32), dt_c * -jnp.exp(alog_c_ref[...]), precision=lax.Precision.HIGHEST)
    acum_r = _dot(dt_r * -jnp.exp(alog_r_ref[...]), (ti <= tj).astype(F32), precision=lax.Precision.HIGHEST)

    first = lax.broadcasted_iota(jnp.int32, (q, LANES), 1) < SSM_HEAD_DIM
    first_row = lax.broadcasted_iota(jnp.int32, (2 * SSM_HEAD_DIM, SSM_STATE), 0) < SSM_HEAD_DIM
    for g in range(N_GROUPS):
        lo = SSM_WIDTH + g * SSM_STATE
        bg = xc[:, lo:lo + SSM_STATE].astype(BF16)
        lo = SSM_WIDTH + (N_GROUPS + g) * SSM_STATE
        cg = xc[:, lo:lo + SSM_STATE].astype(BF16)
        cb = _dot_nt(cg, bg)
        for pair in range(g * HEADS_PER_GROUP // 2, (g + 1) * HEADS_PER_GROUP // 2):
            heads = (2 * pair, 2 * pair + 1)
            ac = [jnp.broadcast_to(acum_c[:, h:h + 1], (q, LANES)) for h in heads]
            dt = [jnp.broadcast_to(dt_c[:, h:h + 1], (q, LANES)) for h in heads]
            m = [(cb * jnp.exp(jnp.where(causal, ac[k] - acum_r[h:h + 1, :], NEG_BIG))).astype(BF16)
                 for k, h in enumerate(heads)]
            acp = jnp.where(first, ac[0], ac[1])
            dtp = jnp.where(first, dt[0], dt[1])
            lanes = slice(pair * LANES, (pair + 1) * LANES)
            xs = xc[:, lanes]
            xdt = xs * dtp
            y = _dot(m[0], jnp.where(first, xdt, 0.0).astype(BF16))
            y = y + _dot(m[1], jnp.where(first, 0.0, xdt).astype(BF16))
            hst = hout_ref[pair]
            y = y + _dot_nt(cg, hst.astype(BF16)) * jnp.exp(acp)
            y = y + dskip_ref[:, lanes] * xs
            xw = (xs * (jnp.exp(acp[q - 1:q, :] - acp) * dtp)).astype(BF16)
            keep = jnp.where(first_row, jnp.exp(acum_r[heads[0]:heads[0] + 1, q - 1:q]),
                             jnp.exp(acum_r[heads[1]:heads[1] + 1, q - 1:q]))
            hout_ref[pair] = hst * keep + _dot_tn(xw, bg)
            y_ref[:, lanes] = y[:rows_in]


def _ssd(x3, col_blk, conv0, h0, dtc, dtc_blk, dtr, conv_w, conv_b, dt_bias, a_log, d_skip, rows_in, valid):
    batch, length = x3.shape[0], x3.shape[1]
    nc = max(length // SSD_CHUNK, 1)
    lane_pad = (0, LANES - N_SSM_HEADS)
    dtb_c = jnp.pad(dt_bias, lane_pad)[None, :]
    alog_c = jnp.pad(a_log, lane_pad)[None, :]
    dtb_r = jnp.broadcast_to(dt_bias[:, None], (N_SSM_HEADS, LANES))
    alog_r = jnp.broadcast_to(a_log[:, None], (N_SSM_HEADS, LANES))
    dskip = jnp.repeat(d_skip, SSM_HEAD_DIM)[None, :]
    const2 = lambda b, c: (0, 0)
    pairs = N_SSM_HEADS // 2
    state_shape = (batch, pairs, 2 * SSM_HEAD_DIM, SSM_STATE)
    state_spec = pl.BlockSpec((None, pairs, 2 * SSM_HEAD_DIM, SSM_STATE), lambda b, c: (b, 0, 0, 0))
    y, h = pl.pallas_call(
        functools.partial(_ssd_kernel, rows_in=rows_in, valid=valid),
        out_shape=(jax.ShapeDtypeStruct((batch, length, SSM_WIDTH), F32),
                   jax.ShapeDtypeStruct(state_shape, F32)),
        grid=(batch, nc),
        in_specs=[pl.BlockSpec((None, rows_in, CONV_DIM), lambda b, c: (b, c, col_blk)),
                  pl.BlockSpec((None, SUBLANES, CONV_DIM), lambda b, c: (b, 0, 0)),
                  state_spec,
                  pl.BlockSpec((None, SSD_CHUNK, LANES), lambda b, c: (b, c, dtc_blk)),
                  pl.BlockSpec((None, N_SSM_HEADS, SSD_CHUNK), lambda b, c: (b, 0, c)),
                  pl.BlockSpec((CONV_WIDTH, CONV_DIM), const2),
                  pl.BlockSpec((1, CONV_DIM), const2),
                  pl.BlockSpec((1, LANES), const2),
                  pl.BlockSpec((1, LANES), const2),
                  pl.BlockSpec((N_SSM_HEADS, LANES), const2),
                  pl.BlockSpec((N_SSM_HEADS, LANES), const2),
                  pl.BlockSpec((1, SSM_WIDTH), const2)],
        out_specs=(pl.BlockSpec((None, rows_in, SSM_WIDTH), lambda b, c: (b, c, 0)), state_spec),
        scratch_shapes=[pltpu.VMEM((SSD_CHUNK + SUBLANES, CONV_DIM), F32)],
        compiler_params=_params("parallel", "arbitrary"),
        name="ssd_mixer",
    )(x3, conv0, h0.reshape(state_shape), dtc, dtr, conv_w, conv_b[None, :], dtb_c, alog_c, dtb_r, alog_r, dskip)
    return y, h.reshape(batch, N_SSM_HEADS, SSM_HEAD_DIM, SSM_STATE)


def _layer_norm(u, g, b):
    mu = jnp.mean(u, axis=-1, keepdims=True)
    d = u - mu
    var = jnp.mean(d * d, axis=-1, keepdims=True)
    return d * lax.rsqrt(var + LN_EPS) * g + b


def _mix_out_kernel(oa_ref, ys_ref, z_ref, x_ref, ga_ref, gs_ref, wo_ref, g1_ref, b1_ref, wr_ref,
                    x1_ref, x1b_ref, lg_ref):
    oa = oa_ref[...]
    na = oa * lax.rsqrt(jnp.mean(oa * oa, axis=-1, keepdims=True) + LN_EPS) * ga_ref[...]
    m = _dot(na.astype(BF16), wo_ref[0:ATTN_WIDTH, :])
    yz = ys_ref[...] * _silu(z_ref[...])
    gw = SSM_WIDTH // N_GROUPS
    gs = gs_ref[...]
    for g in range(N_GROUPS):
        yg = yz[:, g * gw:(g + 1) * gw]
        ng = yg * lax.rsqrt(jnp.mean(yg * yg, axis=-1, keepdims=True) + LN_EPS) * gs[:, g * gw:(g + 1) * gw]
        lo = ATTN_WIDTH + g * gw
        m = m + _dot(ng.astype(BF16), wo_ref[lo:lo + gw, :])
    x1 = _layer_norm(ALPHA * x_ref[...] + m, g1_ref[...], b1_ref[...])
    x1_ref[...] = x1
    x1b = x1.astype(BF16)
    x1b_ref[...] = x1b
    lg_ref[...] = _dot_nt(wr_ref[...], x1b)


def _mix_out(o_attn, y_ssd, proj, x, g_attn, g_ssd, w_out, ln_g, ln_b, w_router_t, tm):
    m = x.shape[0]
    row = lambda i: (i, 0)
    const = lambda i: (0, 0)
    return pl.pallas_call(
        _mix_out_kernel,
        out_shape=(jax.ShapeDtypeStruct((m, D_MODEL), F32),
                   jax.ShapeDtypeStruct((m, D_MODEL), BF16),
                   jax.ShapeDtypeStruct((N_EXPERTS, m), F32)),
        grid=(m // tm,),
        in_specs=[pl.BlockSpec((tm, ATTN_WIDTH), row),
                  pl.BlockSpec((tm, SSM_WIDTH), row),
                  pl.BlockSpec((tm, SSM_WIDTH), lambda i: (i, 3)),
                  pl.BlockSpec((tm, D_MODEL), row),
                  pl.BlockSpec((1, ATTN_WIDTH), const),
                  pl.BlockSpec((1, SSM_WIDTH), const),
                  pl.BlockSpec((D_MODEL, D_MODEL), const),
                  pl.BlockSpec((1, D_MODEL), const),
                  pl.BlockSpec((1, D_MODEL), const),
                  pl.BlockSpec((N_EXPERTS, D_MODEL), const)],
        out_specs=(pl.BlockSpec((tm, D_MODEL), row),
                   pl.BlockSpec((tm, D_MODEL), row),
                   pl.BlockSpec((N_EXPERTS, tm), lambda i: (0, i))),
        compiler_params=_params("parallel"),
        name="mix_out_ln",
    )(o_attn, y_ssd, proj, x, g_attn[None, :], g_ssd[None, :], w_out, ln_g[None, :], ln_b[None, :], w_router_t)


def _route_kernel(lg_ref, bias_ref, uo_ref, idx_ref, gate_ref, rank_ref, cnt_ref):
    per = EXPERTS_PER_GROUP
    tile = ROUTE_TILE
    ninf = -jnp.inf

    @pl.when(pl.program_id(0) == 0)
    def _():
        cnt_ref[...] = jnp.zeros_like(cnt_ref)

    scores = 1.0 / (1.0 + jnp.exp(-lg_ref[...]))
    choice = scores + bias_ref[...]
    sub = lax.broadcasted_iota(jnp.int32, (per, tile), 0)

    def first_max(x):
        m = jnp.max(x, axis=0, keepdims=True)
        return m, jnp.min(jnp.where(x == m, sub, per), axis=0, keepdims=True)

    sc = [scores[g * per:(g + 1) * per] for g in range(N_EXPERT_GROUPS)]
    ch = [choice[g * per:(g + 1) * per] for g in range(N_EXPERT_GROUPS)]

    gscore = jnp.zeros((N_EXPERT_GROUPS, tile), F32)
    for g in range(N_EXPERT_GROUPS):
        m1, i1 = first_max(ch[g])
        m2 = jnp.max(jnp.where(sub == i1, ninf, ch[g]), axis=0, keepdims=True)
        gscore = jnp.where(sub == g, m1 + m2, gscore)
    picked = jnp.zeros((N_EXPERT_GROUPS, tile), F32)
    for _ in range(TOPK_GROUPS):
        _, ig = first_max(gscore)
        hit = sub == ig
        picked = jnp.where(hit, 1.0, picked)
        gscore = jnp.where(hit, ninf, gscore)
    mc = [jnp.where(picked[g:g + 1, :] > 0.5, ch[g], ninf) for g in range(N_EXPERT_GROUPS)]

    eidx = [sub + per * g for g in range(N_EXPERT_GROUPS)]
    onehot = [jnp.zeros((per, tile), F32)] * N_EXPERT_GROUPS
    picks, weights = [], []
    for _ in range(TOP_K):
        mx = mc[0]
        for g in range(1, N_EXPERT_GROUPS):
            mx = jnp.maximum(mx, mc[g])
        m = jnp.max(mx, axis=0, keepdims=True)
        cand = jnp.where(mc[0] == m, eidx[0], N_EXPERTS)
        for g in range(1, N_EXPERT_GROUPS):
            cand = jnp.minimum(cand, jnp.where(mc[g] == m, eidx[g], N_EXPERTS))
        ix = jnp.min(cand, axis=0, keepdims=True)
        w = jnp.zeros((per, tile), F32)
        for g in range(N_EXPERT_GROUPS):
            hit = eidx[g] == ix
            w = w + jnp.where(hit, sc[g], 0.0)
            mc[g] = jnp.where(hit, ninf, mc[g])
            onehot[g] = jnp.where(hit, 1.0, onehot[g])
        picks.append(ix)
        weights.append(jnp.sum(w, axis=0, keepdims=True))
    wsum = weights[0]
    for k in range(1, TOP_K):
        wsum = wsum + weights[k]

    lc = _dot(jnp.concatenate(onehot, axis=0).astype(BF16), uo_ref[...])
    before = lc[:, :tile] + cnt_ref[...]
    cnt_ref[...] += lc[:, tile:]
    idx = jnp.zeros((TOP_K, tile), jnp.int32)
    gate = jnp.zeros((TOP_K, tile), F32)
    rank = jnp.zeros((TOP_K, tile), F32)
    for k in range(TOP_K):
        r = jnp.zeros((per, tile), F32)
        for g in range(N_EXPERT_GROUPS):
            r = r + jnp.where(eidx[g] == picks[k], before[g * per:(g + 1) * per], 0.0)
        idx = jnp.where(sub == k, picks[k], idx)
        gate = jnp.where(sub == k, weights[k] / wsum * ROUTED_SCALE, gate)
        rank = jnp.where(sub == k, jnp.sum(r, axis=0, keepdims=True), rank)
    idx_ref[...] = idx
    gate_ref[...] = gate
    rank_ref[...] = rank.astype(jnp.int32)


def _route(logits_t, router_bias):
    m = logits_t.shape[1]
    tile = ROUTE_TILE
    j = lax.broadcasted_iota(jnp.int32, (tile, tile), 0)
    t = lax.broadcasted_iota(jnp.int32, (tile, tile), 1)
    uo = jnp.concatenate([(j < t).astype(BF16), jnp.ones((tile, tile), BF16)], axis=1)
    bias = jnp.broadcast_to(router_bias[:, None], (N_EXPERTS, tile))
    col = lambda i: (0, i)
    const = lambda i: (0, 0)
    idx, gate, rank, cnt = pl.pallas_call(
        _route_kernel,
        out_shape=(jax.ShapeDtypeStruct((TOP_K, m), jnp.int32),
                   jax.ShapeDtypeStruct((TOP_K, m), F32),
                   jax.ShapeDtypeStruct((TOP_K, m), jnp.int32),
                   jax.ShapeDtypeStruct((N_EXPERTS, tile), F32)),
        grid=(m // tile,),
        in_specs=[pl.BlockSpec((N_EXPERTS, tile), col),
                  pl.BlockSpec((N_EXPERTS, tile), const),
                  pl.BlockSpec((tile, 2 * tile), const)],
        out_specs=(pl.BlockSpec((TOP_K, tile), col),
                   pl.BlockSpec((TOP_K, tile), col),
                   pl.BlockSpec((TOP_K, tile), col),
                   pl.BlockSpec((N_EXPERTS, tile), const)),
        compiler_params=_params("arbitrary"),
        name="moe_route",
    )(logits_t, bias, uo)
    return idx, gate, rank, cnt[:, 0].astype(jnp.int32)


def _dispatch(idx, rank, counts):
    m = idx.shape[1]
    r = MOE_ROWS
    padded = (counts + r - 1) // r * r
    pend = jnp.cumsum(padded)
    pstart = pend - padded
    dest = pstart[idx] + rank
    nb = -(-(m * TOP_K) // r) + N_EXPERTS
    blk_lo = jnp.arange(nb, dtype=jnp.int32) * r
    blk_e = jnp.minimum(jnp.sum(pend[None, :] <= blk_lo[:, None], axis=1), N_EXPERTS - 1).astype(jnp.int32)
    nvalid = jnp.clip((pstart + counts)[blk_e] - blk_lo, 0, r).astype(jnp.int32)
    tok = jnp.broadcast_to(jnp.arange(m, dtype=jnp.int32)[None, :], (TOP_K, m))
    tok_buf = jnp.full((nb * r,), m, jnp.int32).at[dest.reshape(-1)].set(tok.reshape(-1))
    return dest, tok_buf, blk_e, nvalid


def _moe_kernel(blk_e_ref, nvalid_ref, x_ref, wg_ref, wu_ref, wd_ref, o_ref, wgb, wub, wdb):
    i = pl.program_id(0)
    e = blk_e_ref[i]
    prev = blk_e_ref[jnp.maximum(i - 1, 0)]

    @pl.when((i == 0) | (e != prev))
    def _():
        wgb[...] = wg_ref[...].astype(BF16)
        wub[...] = wu_ref[...].astype(BF16)
        wdb[...] = wd_ref[...].astype(BF16)

    @pl.when(nvalid_ref[i] > 0)
    def _():
        x = x_ref[...]
        act = (_silu(_dot(x, wgb[...])) * _dot(x, wub[...])).astype(BF16)
        o_ref[...] = _dot(act, wdb[...]).astype(BF16)

    @pl.when(nvalid_ref[i] == 0)
    def _():
        o_ref[...] = jnp.zeros_like(o_ref)


def _moe_experts(blk_e, nvalid, xg, w_gate_e, w_up_e, w_down_e):
    cap = xg.shape[0]
    r = MOE_ROWS
    grid_spec = pltpu.PrefetchScalarGridSpec(
        num_scalar_prefetch=2,
        grid=(cap // r,),
        in_specs=[pl.BlockSpec((r, D_MODEL), lambda i, be, nv: (i, 0)),
                  pl.BlockSpec((None, D_MODEL, D_EXPERT), lambda i, be, nv: (be[i], 0, 0)),
                  pl.BlockSpec((None, D_MODEL, D_EXPERT), lambda i, be, nv: (be[i], 0, 0)),
                  pl.BlockSpec((None, D_EXPERT, D_MODEL), lambda i, be, nv: (be[i], 0, 0))],
        out_specs=pl.BlockSpec((r, D_MODEL), lambda i, be, nv: (i, 0)),
        scratch_shapes=[pltpu.VMEM((D_MODEL, D_EXPERT), BF16),
                        pltpu.VMEM((D_MODEL, D_EXPERT), BF16),
                        pltpu.VMEM((D_EXPERT, D_MODEL), BF16)])
    return pl.pallas_call(
        _moe_kernel,
        out_shape=jax.ShapeDtypeStruct((cap, D_MODEL), BF16),
        grid_spec=grid_spec,
        compiler_params=_params("arbitrary"),
        name="moe_experts",
    )(blk_e, nvalid, xg, w_gate_e, w_up_e, w_down_e)


def _final_kernel(x1_ref, x1b_ref, routed_ref, wg_ref, wu_ref, wd_ref, g_ref, b_ref, o_ref):
    xb = x1b_ref[...]
    act = (_silu(_dot(xb, wg_ref[...])) * _dot(xb, wu_ref[...])).astype(BF16)
    moe = routed_ref[...] + _dot(act, wd_ref[...])
    o_ref[...] = _layer_norm(ALPHA * x1_ref[...] + moe, g_ref[...], b_ref[...])


def _final(x1, x1b, routed, row_off, w_gate_s, w_up_s, w_down_s, ln_g, ln_b, tm):
    m = x1.shape[0]
    off = row_off // tm
    row = lambda i: (i, 0)
    const = lambda i: (0, 0)
    return pl.pallas_call(
        _final_kernel,
        out_shape=jax.ShapeDtypeStruct((m, D_MODEL), F32),
        grid=(m // tm,),
        in_specs=[pl.BlockSpec((tm, D_MODEL), row),
                  pl.BlockSpec((tm, D_MODEL), row),
                  pl.BlockSpec((tm, D_MODEL), lambda i: (i + off, 0)),
                  pl.BlockSpec((D_MODEL, D_EXPERT), const),
                  pl.BlockSpec((D_MODEL, D_EXPERT), const),
                  pl.BlockSpec((D_EXPERT, D_MODEL), const),
                  pl.BlockSpec((1, D_MODEL), const),
                  pl.BlockSpec((1, D_MODEL), const)],
        out_specs=pl.BlockSpec((tm, D_MODEL), row),
        compiler_params=_params("parallel"),
        name="shared_ln",
    )(x1, x1b, routed, w_gate_s, w_up_s, w_down_s, ln_g[None, :], ln_b[None, :])


def kernel(x_prompt, x_sample, cache_k, cache_v, state_conv, state_ssm, page_table, w_in, sb_bias, conv_w, conv_b, dt_bias, a_log, d_skip, g_attn, g_ssd, w_out, ln1_g, ln1_b, w_router, router_bias, w_gate_e, w_up_e, w_down_e, w_gate_s, w_up_s, w_down_s, ln2_g, ln2_b):
    assert w_in.shape[0] == 1, "one layer"
    bp, seq, _ = x_prompt.shape
    bs, dec, _ = x_sample.shape
    assert dec >= CONV_WIDTH - 1
    mp, ms = bp * seq, bs * dec
    n_phys = cache_k.shape[1]

    w_in_b = jnp.pad(w_in[0], ((0, 0), (0, IN_PAD - w_in.shape[2]))).astype(BF16)
    w_out_b = w_out[0].astype(BF16)
    w_router_t = w_router[0].T.astype(BF16)
    mix_w = (g_attn[0], g_ssd[0], w_out_b, ln1_g[0], ln1_b[0], w_router_t)
    ssd_w = (conv_w[0], conv_b[0], dt_bias[0], a_log[0], d_skip[0])
    dt_blk = IN_MAIN // LANES
    xbc_blk = (3 * ATTN_WIDTH + SSM_WIDTH) // CONV_DIM

    xp = x_prompt.reshape(mp, D_MODEL)
    proj_p = _proj(xp, w_in_b, 512, 896)
    proj_p3 = proj_p.reshape(bp, seq, IN_PAD)
    k_prompt = proj_p3[:, :, ATTN_WIDTH:2 * ATTN_WIDTH].reshape(1, bp, seq, N_HEADS, HEAD_DIM)
    v_prompt = proj_p3[:, :, 2 * ATTN_WIDTH:3 * ATTN_WIDTH].reshape(1, bp, seq, N_HEADS, HEAD_DIM)
    conv_prompt = proj_p3[:, seq - (CONV_WIDTH - 1):, IN_MAIN - CONV_DIM:IN_MAIN][None]
    oa_p = _attn_prompt(proj_p, sb_bias[0], bp, seq)
    dtr_p = proj_p3[:, :, IN_MAIN:IN_MAIN + N_SSM_HEADS].transpose(0, 2, 1)
    y_p, h_p = _ssd(proj_p3, xbc_blk, jnp.zeros((bp, SUBLANES, CONV_DIM), F32),
                    jnp.zeros((bp, N_SSM_HEADS, SSM_HEAD_DIM, SSM_STATE), F32),
                    proj_p3, dt_blk, dtr_p, *ssd_w, rows_in=SSD_CHUNK, valid=SSD_CHUNK)
    x1_p, x1b_p, lg_p = _mix_out(oa_p, y_p.reshape(mp, SSM_WIDTH), proj_p, xp, *mix_w, tm=256)

    xs = x_sample.reshape(ms, D_MODEL)
    proj_s = _proj(xs, w_in_b, ms, 896)
    proj_s3 = proj_s.reshape(bs, dec, IN_PAD)
    k_sample = proj_s3[:, :, ATTN_WIDTH:2 * ATTN_WIDTH].reshape(1, bs, dec, N_HEADS, HEAD_DIM)
    v_sample = proj_s3[:, :, 2 * ATTN_WIDTH:3 * ATTN_WIDTH].reshape(1, bs, dec, N_HEADS, HEAD_DIM)
    conv_sample = proj_s3[:, dec - (CONV_WIDTH - 1):, IN_MAIN - CONV_DIM:IN_MAIN][None]
    proj_s8 = jnp.pad(proj_s3, ((0, 0), (0, SUBLANES - dec), (0, 0)))
    oa_s = _attn_sample(proj_s8, cache_k.reshape(n_phys, PAGE_SIZE * N_HEADS, HEAD_DIM),
                        cache_v.reshape(n_phys, PAGE_SIZE * N_HEADS, HEAD_DIM), page_table, sb_bias[0])
    dtc_s = jnp.pad(proj_s8[:, :, IN_MAIN:], ((0, 0), (0, SSD_CHUNK - SUBLANES), (0, 0)))
    dtr_s = dtc_s[:, :, :N_SSM_HEADS].transpose(0, 2, 1)
    conv0_s = jnp.pad(state_conv[0], ((0, 0), (SUBLANES - (CONV_WIDTH - 1), 0), (0, 0)))
    y_s, h_s = _ssd(proj_s8, xbc_blk, conv0_s, state_ssm[0], dtc_s, 0, dtr_s, *ssd_w,
                    rows_in=SUBLANES, valid=dec)
    x1_s, x1b_s, lg_s = _mix_out(oa_s[:, :dec].reshape(ms, ATTN_WIDTH), y_s[:, :dec].reshape(ms, SSM_WIDTH),
                                 proj_s, xs, *mix_w, tm=ms)

    m = mp + ms
    idx, gates, rank, counts = _route(jnp.concatenate([lg_p, lg_s], axis=1), router_bias[0])
    dest, tok_buf, blk_e, nvalid = _dispatch(idx, rank, counts)
    x1b = jnp.concatenate([x1b_p, x1b_s, jnp.zeros((1, D_MODEL), BF16)], axis=0)
    out_rows = _moe_experts(blk_e, nvalid, x1b[tok_buf], w_gate_e[0], w_up_e[0], w_down_e[0])
    routed = (out_rows[dest.reshape(-1)].reshape(TOP_K, m, D_MODEL).astype(F32) * gates[:, :, None]).sum(axis=0)
    shared_w = (w_gate_s[0].astype(BF16), w_up_s[0].astype(BF16), w_down_s[0].astype(BF16), ln2_g[0], ln2_b[0])
    y_p = _final(x1_p, x1b_p, routed, 0, *shared_w, tm=256)
    y_s = _final(x1_s, x1b_s, routed, mp, *shared_w, tm=ms)

    return (y_p.reshape(bp, seq, D_MODEL), y_s.reshape(bs, dec, D_MODEL),
            k_prompt, v_prompt, conv_prompt, h_p[None],
            k_sample, v_sample, conv_sample, h_s[None])
```

```python
import functools

import jax
import jax.numpy as jnp
from jax import lax
from jax.experimental import pallas as pl
from jax.experimental.pallas import tpu as pltpu

F32 = jnp.float32
BF16 = jnp.bfloat16

D_MODEL = 2048
ATTN_WIDTH = 1024
SSM_WIDTH = 1024
HEAD_DIM = 128
N_HEADS = 8
SB_SCALE = HEAD_DIM ** -0.5
SSM_HEAD_DIM = 64
N_SSM_HEADS = 16
SSM_STATE = 128
N_GROUPS = 4
HEADS_PER_GROUP = N_SSM_HEADS // N_GROUPS
CONV_WIDTH = 4
CONV_DIM = SSM_WIDTH + 2 * N_GROUPS * SSM_STATE
IN_MAIN = 3 * ATTN_WIDTH + SSM_WIDTH + CONV_DIM
IN_PAD = IN_MAIN + 128
N_EXPERTS = 64
N_EXPERT_GROUPS = 8
EXPERTS_PER_GROUP = N_EXPERTS // N_EXPERT_GROUPS
TOPK_GROUPS = 4
TOP_K = 8
D_EXPERT = 512
ROUTED_SCALE = 2.5
ALPHA = 2.0 ** 0.25
LN_EPS = 1e-5
PAGE_SIZE = 128

LANES = 128
SUBLANES = 8
VMEM_LIMIT = 56 * 1024 * 1024

SSD_CHUNK = 128
ATTN_BLOCK = 256
ATTN_HEADS_PER_STEP = 4
PAGES_PER_STEP = 4
MOE_ROWS = 256
ROUTE_TILE = 128
ROW_TILE = 64
NEG_BIG = -1e30
LOG2E = 1.4426950408889634


def _params(*sem):
    return pltpu.CompilerParams(dimension_semantics=sem, vmem_limit_bytes=VMEM_LIMIT)


def _softplus(x):
    return jnp.maximum(x, 0.0) + jnp.log1p(jnp.exp(-jnp.abs(x)))


def _silu(x):
    return x * (1.0 / (1.0 + jnp.exp(-x)))


def _dot(a, b, **kw):
    return jnp.dot(a, b, preferred_element_type=F32, **kw)


def _dot_nt(a, b):
    return lax.dot_general(a, b, (((1,), (1,)), ((), ())), preferred_element_type=F32)


def _dot_tn(a, b):
    return lax.dot_general(a, b, (((0,), (0,)), ((), ())), preferred_element_type=F32)


def _proj_kernel(x_ref, w_ref, o_ref):
    o_ref[...] = _dot(x_ref[...].astype(BF16), w_ref[...])


def _proj(x, w, tm, tn):
    m, d = x.shape
    n = w.shape[1]
    return pl.pallas_call(
        _proj_kernel,
        out_shape=jax.ShapeDtypeStruct((m, n), F32),
        grid=(m // tm, n // tn),
        in_specs=[pl.BlockSpec((tm, d), lambda i, j: (i, 0)),
                  pl.BlockSpec((d, tn), lambda i, j: (0, j))],
        out_specs=pl.BlockSpec((tm, tn), lambda i, j: (i, j)),
        compiler_params=_params("parallel", "arbitrary"),
        name="in_proj",
    )(x, w)


def _neg_suffix(n):
    j = lax.broadcasted_iota(jnp.int32, (n, n), 0)
    s = lax.broadcasted_iota(jnp.int32, (n, n), 1)
    return -jnp.concatenate([(j > s).astype(BF16), jnp.ones((n, LANES), BF16)], axis=1)


def _sb_block(z, valid, c, uo, nk):
    sp = jnp.maximum(z, 0.0) + jnp.log(1.0 + jnp.exp2(jnp.abs(z) * -LOG2E))
    spm = sp if valid is None else jnp.where(valid, sp, 0.0)
    lc = _dot(spm.astype(BF16), uo)
    cb = c if nk == LANES else jnp.concatenate([c] * (nk // LANES), axis=1)
    a = jnp.exp2(((z - sp) + lc[:, :nk] + cb) * LOG2E)
    if valid is not None:
        a = jnp.where(valid, a, 0.0)
    return a, c + lc[:, nk:]


def _attn_prompt_kernel(bias_ref, q_ref, k_ref, v_ref, uo_ref, o_ref, kb_ref, vb_ref, *, blk, heads):
    hg = pl.program_id(1)
    i = pl.program_id(2)

    @pl.when(i == 0)
    def _():
        kb_ref[...] = k_ref[...].astype(BF16)
        vb_ref[...] = v_ref[...].astype(BF16)

    uo = uo_ref[...]
    row = lax.broadcasted_iota(jnp.int32, (blk, blk), 0)
    col = lax.broadcasted_iota(jnp.int32, (blk, blk), 1)
    qs = [q_ref[:, h * HEAD_DIM:(h + 1) * HEAD_DIM].astype(BF16) for h in range(heads)]
    biases = [bias_ref[hg * heads + h] for h in range(heads)]

    def block(j, carry, valid):
        start = pl.multiple_of(j * blk, blk)
        out = []
        for h in range(heads):
            c, acc = carry[h]
            lanes = slice(h * HEAD_DIM, (h + 1) * HEAD_DIM)
            z = _dot_nt(qs[h], kb_ref[pl.ds(start, blk), lanes]) * SB_SCALE + biases[h]
            a, c = _sb_block(z, valid, c, uo, blk)
            out.append((c, acc + _dot(a.astype(BF16), vb_ref[pl.ds(start, blk), lanes])))
        return tuple(out)

    zero = jnp.zeros((blk, HEAD_DIM), F32)
    carry = block(i, ((zero, zero),) * heads, col < row)
    carry = lax.fori_loop(0, i, lambda jj, cr: block(i - 1 - jj, cr, None), carry)
    for h in range(heads):
        o_ref[:, h * HEAD_DIM:(h + 1) * HEAD_DIM] = carry[h][1]


def _attn_prompt(proj, sb_bias, batch, seq):
    blk = ATTN_BLOCK
    heads = ATTN_HEADS_PER_STEP
    nq = seq // blk
    ng = N_HEADS // heads
    width = heads * HEAD_DIM
    return pl.pallas_call(
        functools.partial(_attn_prompt_kernel, blk=blk, heads=heads),
        out_shape=jax.ShapeDtypeStruct((batch * seq, ATTN_WIDTH), F32),
        grid=(batch, ng, nq),
        in_specs=[pl.BlockSpec(memory_space=pltpu.SMEM),
                  pl.BlockSpec((blk, width), lambda b, g, i: (b * nq + i, g)),
                  pl.BlockSpec((seq, width), lambda b, g, i: (b, ng + g)),
                  pl.BlockSpec((seq, width), lambda b, g, i: (b, 2 * ng + g)),
                  pl.BlockSpec((blk, blk + LANES), lambda b, g, i: (0, 0))],
        out_specs=pl.BlockSpec((blk, width), lambda b, g, i: (b * nq + i, g)),
        scratch_shapes=[pltpu.VMEM((seq, width), BF16), pltpu.VMEM((seq, width), BF16)],
        compiler_params=_params("parallel", "parallel", "arbitrary"),
        name="sb_attn_prompt",
    )(sb_bias, proj, proj, proj, _neg_suffix(blk))


def _attn_sample_kernel(pt_ref, bias_ref, q_ref, kn_ref, vn_ref, *rest, pps):
    kp_refs, vp_refs = rest[:pps], rest[pps:2 * pps]
    uo_ref, o_ref, qb_ref, bias_v, c_ref, acc_ref = rest[2 * pps:]
    s = pl.program_id(1)
    rows = N_HEADS * SUBLANES
    uo = uo_ref[...]

    def sweep(blocks, valid):
        zs = [jnp.concatenate([_dot_nt(qb_ref[:, h * HEAD_DIM:(h + 1) * HEAD_DIM], load_k(h))
                               for h in range(N_HEADS)], axis=0) * SB_SCALE + bias_v[...]
              for load_k, _ in blocks]
        c = c_ref[...]
        acc = acc_ref[...]
        for z, (_, load_v) in zip(zs, blocks):
            a, c = _sb_block(z, valid, c, uo, PAGE_SIZE)
            acc = acc + jnp.concatenate(
                [_dot(a[h * SUBLANES:(h + 1) * SUBLANES].astype(BF16), load_v(h)) for h in range(N_HEADS)], axis=0)
        c_ref[...] = c
        acc_ref[...] = acc

    @pl.when(s == 0)
    def _():
        qb_ref[...] = q_ref[...].astype(BF16)
        rh = lax.broadcasted_iota(jnp.int32, (rows, PAGE_SIZE), 0) // SUBLANES
        bv = jnp.zeros((rows, PAGE_SIZE), F32)
        for h in range(N_HEADS):
            bv = jnp.where(rh == h, bias_ref[h], bv)
        bias_v[...] = bv
        c_ref[...] = jnp.zeros_like(c_ref)
        acc_ref[...] = jnp.zeros_like(acc_ref)
        pad = jnp.zeros((PAGE_SIZE - SUBLANES, HEAD_DIM), F32)

        def new_rows(ref):
            return lambda h: jnp.concatenate([ref[:, h * HEAD_DIM:(h + 1) * HEAD_DIM], pad], axis=0).astype(BF16)

        t = lax.broadcasted_iota(jnp.int32, (rows, PAGE_SIZE), 0) % SUBLANES
        key = lax.broadcasted_iota(jnp.int32, (rows, PAGE_SIZE), 1)
        sweep([(new_rows(kn_ref), new_rows(vn_ref))], key < t)

    def page_rows(ref):
        return lambda h: ref[pl.ds(h, PAGE_SIZE, stride=N_HEADS), :].astype(BF16)

    sweep([(page_rows(kp_refs[u]), page_rows(vp_refs[u])) for u in range(pps)], None)

    @pl.when(s == pl.num_programs(1) - 1)
    def _():
        for h in range(N_HEADS):
            o_ref[:, h * HEAD_DIM:(h + 1) * HEAD_DIM] = acc_ref[h * SUBLANES:(h + 1) * SUBLANES, :]


def _attn_sample(proj8, cache_k, cache_v, page_table, sb_bias):
    batch = proj8.shape[0]
    n_pages = page_table.shape[1]
    pps = PAGES_PER_STEP
    steps = n_pages // pps
    rows = N_HEADS * SUBLANES

    def tok_spec(col):
        return pl.BlockSpec((None, SUBLANES, ATTN_WIDTH), lambda b, s, pt: (b, 0, col))

    def page_spec(u):
        return pl.BlockSpec((None, PAGE_SIZE * N_HEADS, HEAD_DIM),
                            lambda b, s, pt: (pt[b, n_pages - 1 - (s * pps + u)], 0, 0))

    grid_spec = pltpu.PrefetchScalarGridSpec(
        num_scalar_prefetch=1,
        grid=(batch, steps),
        in_specs=[pl.BlockSpec(memory_space=pltpu.SMEM), tok_spec(0), tok_spec(1), tok_spec(2)]
        + [page_spec(u) for u in range(pps)] * 2
        + [pl.BlockSpec((PAGE_SIZE, 2 * LANES), lambda b, s, pt: (0, 0))],
        out_specs=pl.BlockSpec((None, SUBLANES, ATTN_WIDTH), lambda b, s, pt: (b, 0, 0)),
        scratch_shapes=[pltpu.VMEM((SUBLANES, ATTN_WIDTH), BF16),
                        pltpu.VMEM((rows, PAGE_SIZE), F32),
                        pltpu.VMEM((rows, LANES), F32),
                        pltpu.VMEM((rows, HEAD_DIM), F32)])
    return pl.pallas_call(
        functools.partial(_attn_sample_kernel, pps=pps),
        out_shape=jax.ShapeDtypeStruct((batch, SUBLANES, ATTN_WIDTH), F32),
        grid_spec=grid_spec,
        compiler_params=_params("parallel", "arbitrary"),
        name="sb_attn_sample",
    )(page_table, sb_bias, proj8, proj8, proj8,
      *([cache_k] * pps), *([cache_v] * pps), _neg_suffix(PAGE_SIZE))


def _ssd_kernel(x_ref, conv0_ref, h0_ref, dtc_ref, dtr_ref, cw_ref, cb_ref,
                dtb_c_ref, alog_c_ref, dtb_r_ref, alog_r_ref, dskip_ref, y_ref, hout_ref, ext_ref,
                *, rows_in, valid):
    q = SSD_CHUNK
    c = pl.program_id(1)

    @pl.when(c == 0)
    def _():
        ext_ref[0:SUBLANES, :] = conv0_ref[...]
        hout_ref[...] = h0_ref[...]

    if rows_in < q:
        ext_ref[SUBLANES:, :] = jnp.zeros((q, CONV_DIM), F32)
    ext_ref[SUBLANES:SUBLANES + rows_in, :] = x_ref[...]
    cw = cw_ref[...]
    xc = cb_ref[...] + ext_ref[5:5 + q, :] * cw[0:1, :]
    xc = xc + ext_ref[6:6 + q, :] * cw[1:2, :]
    xc = xc + ext_ref[7:7 + q, :] * cw[2:3, :]
    xc = _silu(xc + ext_ref[8:8 + q, :] * cw[3:4, :])
    ext_ref[0:SUBLANES, :] = ext_ref[q:q + SUBLANES, :]

    t_c = lax.broadcasted_iota(jnp.int32, (q, LANES), 0)
    h_c = lax.broadcasted_iota(jnp.int32, (q, LANES), 1)
    dt_c = jnp.where((t_c < valid) & (h_c < N_SSM_HEADS), _softplus(dtc_ref[...] + dtb_c_ref[...]), 0.0)
    t_r = lax.broadcasted_iota(jnp.int32, (N_SSM_HEADS, q), 1)
    dt_r = jnp.where(t_r < valid, _softplus(dtr_ref[...] + dtb_r_ref[...]), 0.0)
    ti = lax.broadcasted_iota(jnp.int32, (q, q), 0)
    tj = lax.broadcasted_iota(jnp.int32, (q, q), 1)
    causal = tj <= ti
    acum_c = _dot(causal.astype(F32), dt_c * -jnp.exp(alog_c_ref[...]), precision=lax.Precision.HIGHEST)
    acum_r = _dot(dt_r * -jnp.exp(alog_r_ref[...]), (ti <= tj).astype(F32), precision=lax.Precision.HIGHEST)

    first = lax.broadcasted_iota(jnp.int32, (q, LANES), 1) < SSM_HEAD_DIM
    first_row = lax.broadcasted_iota(jnp.int32, (2 * SSM_HEAD_DIM, SSM_STATE), 0) < SSM_HEAD_DIM
    for g in range(N_GROUPS):
        lo = SSM_WIDTH + g * SSM_STATE
        bg = xc[:, lo:lo + SSM_STATE].astype(BF16)
        lo = SSM_WIDTH + (N_GROUPS + g) * SSM_STATE
        cg = xc[:, lo:lo + SSM_STATE].astype(BF16)
        cb = _dot_nt(cg, bg)
        for pair in range(g * HEADS_PER_GROUP // 2, (g + 1) * HEADS_PER_GROUP // 2):
            heads = (2 * pair, 2 * pair + 1)
            ac = [jnp.broadcast_to(acum_c[:, h:h + 1], (q, LANES)) for h in heads]
            dt = [jnp.broadcast_to(dt_c[:, h:h + 1], (q, LANES)) for h in heads]
            m = [(cb * jnp.exp(jnp.where(causal, ac[k] - acum_r[h:h + 1, :], NEG_BIG))).astype(BF16)
                 for k, h in enumerate(heads)]
            acp = jnp.where(first, ac[0], ac[1])
            dtp = jnp.where(first, dt[0], dt[1])
            lanes = slice(pair * LANES, (pair + 1) * LANES)
            xs = xc[:, lanes]
            xdt = xs * dtp
            y = _dot(m[0], jnp.where(first, xdt, 0.0).astype(BF16))
            y = y + _dot(m[1], jnp.where(first, 0.0, xdt).astype(BF16))
            hst = hout_ref[pair]
            y = y + _dot_nt(cg, hst.astype(BF16)) * jnp.exp(acp)
            y = y + dskip_ref[:, lanes] * xs
            xw = (xs * (jnp.exp(acp[q - 1:q, :] - acp) * dtp)).astype(BF16)
            keep = jnp.where(first_row, jnp.exp(acum_r[heads[0]:heads[0] + 1, q - 1:q]),
                             jnp.exp(acum_r[heads[1]:heads[1] + 1, q - 1:q]))
            hout_ref[pair] = hst * keep + _dot_tn(xw, bg)
            y_ref[:, lanes] = y[:rows_in]


def _ssd(x3, col_blk, conv0, h0, dtc, dtc_blk, dtr, conv_w, conv_b, dt_bias, a_log, d_skip, rows_in, valid):
    batch, length = x3.shape[0], x3.shape[1]
    nc = max(length // SSD_CHUNK, 1)
    lane_pad = (0, LANES - N_SSM_HEADS)
    dtb_c = jnp.pad(dt_bias, lane_pad)[None, :]
    alog_c = jnp.pad(a_log, lane_pad)[None, :]
    dtb_r = jnp.broadcast_to(dt_bias[:, None], (N_SSM_HEADS, LANES))
    alog_r = jnp.broadcast_to(a_log[:, None], (N_SSM_HEADS, LANES))
    dskip = jnp.repeat(d_skip, SSM_HEAD_DIM)[None, :]
    const2 = lambda b, c: (0, 0)
    pairs = N_SSM_HEADS // 2
    state_shape = (batch, pairs, 2 * SSM_HEAD_DIM, SSM_STATE)
    state_spec = pl.BlockSpec((None, pairs, 2 * SSM_HEAD_DIM, SSM_STATE), lambda b, c: (b, 0, 0, 0))
    y, h = pl.pallas_call(
        functools.partial(_ssd_kernel, rows_in=rows_in, valid=valid),
        out_shape=(jax.ShapeDtypeStruct((batch, length, SSM_WIDTH), F32),
                   jax.ShapeDtypeStruct(state_shape, F32)),
        grid=(batch, nc),
        in_specs=[pl.BlockSpec((None, rows_in, CONV_DIM), lambda b, c: (b, c, col_blk)),
                  pl.BlockSpec((None, SUBLANES, CONV_DIM), lambda b, c: (b, 0, 0)),
                  state_spec,
                  pl.BlockSpec((None, SSD_CHUNK, LANES), lambda b, c: (b, c, dtc_blk)),
                  pl.BlockSpec((None, N_SSM_HEADS, SSD_CHUNK), lambda b, c: (b, 0, c)),
                  pl.BlockSpec((CONV_WIDTH, CONV_DIM), const2),
                  pl.BlockSpec((1, CONV_DIM), const2),
                  pl.BlockSpec((1, LANES), const2),
                  pl.BlockSpec((1, LANES), const2),
                  pl.BlockSpec((N_SSM_HEADS, LANES), const2),
                  pl.BlockSpec((N_SSM_HEADS, LANES), const2),
                  pl.BlockSpec((1, SSM_WIDTH), const2)],
        out_specs=(pl.BlockSpec((None, rows_in, SSM_WIDTH), lambda b, c: (b, c, 0)), state_spec),
        scratch_shapes=[pltpu.VMEM((SSD_CHUNK + SUBLANES, CONV_DIM), F32)],
        compiler_params=_params("parallel", "arbitrary"),
        name="ssd_mixer",
    )(x3, conv0, h0.reshape(state_shape), dtc, dtr, conv_w, conv_b[None, :], dtb_c, alog_c, dtb_r, alog_r, dskip)
    return y, h.reshape(batch, N_SSM_HEADS, SSM_HEAD_DIM, SSM_STATE)


def _layer_norm(u, g, b):
    mu = jnp.mean(u, axis=-1, keepdims=True)
    d = u - mu
    var = jnp.mean(d * d, axis=-1, keepdims=True)
    return d * lax.rsqrt(var + LN_EPS) * g + b


def _mix_out_kernel(oa_ref, ys_ref, z_ref, x_ref, ga_ref, gs_ref, wo_ref, g1_ref, b1_ref, wr_ref,
                    x1_ref, lg_ref):
    oa = oa_ref[...]
    na = oa * lax.rsqrt(jnp.mean(oa * oa, axis=-1, keepdims=True) + LN_EPS) * ga_ref[...]
    m = _dot(na.astype(BF16), wo_ref[0:ATTN_WIDTH, :])
    yz = ys_ref[...] * _silu(z_ref[...])
    gw = SSM_WIDTH // N_GROUPS
    gs = gs_ref[...]
    for g in range(N_GROUPS):
        yg = yz[:, g * gw:(g + 1) * gw]
        ng = yg * lax.rsqrt(jnp.mean(yg * yg, axis=-1, keepdims=True) + LN_EPS) * gs[:, g * gw:(g + 1) * gw]
        lo = ATTN_WIDTH + g * gw
        m = m + _dot(ng.astype(BF16), wo_ref[lo:lo + gw, :])
    x1 = _layer_norm(ALPHA * x_ref[...] + m, g1_ref[...], b1_ref[...])
    x1_ref[...] = x1
    lg_ref[...] = _dot_nt(wr_ref[...], x1.astype(BF16))


def _mix_out(o_attn, y_ssd, proj, x, g_attn, g_ssd, w_out, ln_g, ln_b, w_router_t, tm):
    m = x.shape[0]
    row = lambda i: (i, 0)
    const = lambda i: (0, 0)
    return pl.pallas_call(
        _mix_out_kernel,
        out_shape=(jax.ShapeDtypeStruct((m, D_MODEL), F32),
                   jax.ShapeDtypeStruct((N_EXPERTS, m), F32)),
        grid=(m // tm,),
        in_specs=[pl.BlockSpec((tm, ATTN_WIDTH), row),
                  pl.BlockSpec((tm, SSM_WIDTH), row),
                  pl.BlockSpec((tm, SSM_WIDTH), lambda i: (i, 3)),
                  pl.BlockSpec((tm, D_MODEL), row),
                  pl.BlockSpec((1, ATTN_WIDTH), const),
                  pl.BlockSpec((1, SSM_WIDTH), const),
                  pl.BlockSpec((D_MODEL, D_MODEL), const),
                  pl.BlockSpec((1, D_MODEL), const),
                  pl.BlockSpec((1, D_MODEL), const),
                  pl.BlockSpec((N_EXPERTS, D_MODEL), const)],
        out_specs=(pl.BlockSpec((tm, D_MODEL), row),
                   pl.BlockSpec((N_EXPERTS, tm), lambda i: (0, i))),
        compiler_params=_params("parallel"),
        name="mix_out_ln",
    )(o_attn, y_ssd, proj, x, g_attn[None, :], g_ssd[None, :], w_out, ln_g[None, :], ln_b[None, :], w_router_t)


def _route_kernel(lg_ref, bias_ref, uo_ref, idx_ref, gate_ref, rank_ref, cnt_ref):
    per = EXPERTS_PER_GROUP
    tile = ROUTE_TILE
    ninf = -jnp.inf

    @pl.when(pl.program_id(0) == 0)
    def _():
        cnt_ref[...] = jnp.zeros_like(cnt_ref)

    scores = 1.0 / (1.0 + jnp.exp(-lg_ref[...]))
    choice = scores + bias_ref[...]
    sub = lax.broadcasted_iota(jnp.int32, (per, tile), 0)

    def first_max(x):
        m = jnp.max(x, axis=0, keepdims=True)
        return m, jnp.min(jnp.where(x == m, sub, per), axis=0, keepdims=True)

    sc = [scores[g * per:(g + 1) * per] for g in range(N_EXPERT_GROUPS)]
    ch = [choice[g * per:(g + 1) * per] for g in range(N_EXPERT_GROUPS)]

    gscore = jnp.zeros((N_EXPERT_GROUPS, tile), F32)
    for g in range(N_EXPERT_GROUPS):
        m1, i1 = first_max(ch[g])
        m2 = jnp.max(jnp.where(sub == i1, ninf, ch[g]), axis=0, keepdims=True)
        gscore = jnp.where(sub == g, m1 + m2, gscore)
    picked = jnp.zeros((N_EXPERT_GROUPS, tile), F32)
    for _ in range(TOPK_GROUPS):
        _, ig = first_max(gscore)
        hit = sub == ig
        picked = jnp.where(hit, 1.0, picked)
        gscore = jnp.where(hit, ninf, gscore)
    mc = [jnp.where(picked[g:g + 1, :] > 0.5, ch[g], ninf) for g in range(N_EXPERT_GROUPS)]

    eidx = [sub + per * g for g in range(N_EXPERT_GROUPS)]
    onehot = [jnp.zeros((per, tile), F32)] * N_EXPERT_GROUPS
    picks, weights = [], []
    for _ in range(TOP_K):
        mx = mc[0]
        for g in range(1, N_EXPERT_GROUPS):
            mx = jnp.maximum(mx, mc[g])
        m = jnp.max(mx, axis=0, keepdims=True)
        cand = jnp.where(mc[0] == m, eidx[0], N_EXPERTS)
        for g in range(1, N_EXPERT_GROUPS):
            cand = jnp.minimum(cand, jnp.where(mc[g] == m, eidx[g], N_EXPERTS))
        ix = jnp.min(cand, axis=0, keepdims=True)
        w = jnp.zeros((per, tile), F32)
        for g in range(N_EXPERT_GROUPS):
            hit = eidx[g] == ix
            w = w + jnp.where(hit, sc[g], 0.0)
            mc[g] = jnp.where(hit, ninf, mc[g])
            onehot[g] = jnp.where(hit, 1.0, onehot[g])
        picks.append(ix)
        weights.append(jnp.sum(w, axis=0, keepdims=True))
    wsum = weights[0]
    for k in range(1, TOP_K):
        wsum = wsum + weights[k]

    lc = _dot(jnp.concatenate(onehot, axis=0).astype(BF16), uo_ref[...])
    before = lc[:, :tile] + cnt_ref[...]
    cnt_ref[...] += lc[:, tile:]
    idx = jnp.zeros((TOP_K, tile), jnp.int32)
    gate = jnp.zeros((TOP_K, tile), F32)
    rank = jnp.zeros((TOP_K, tile), F32)
    for k in range(TOP_K):
        r = jnp.zeros((per, tile), F32)
        for g in range(N_EXPERT_GROUPS):
            r = r + jnp.where(eidx[g] == picks[k], before[g * per:(g + 1) * per], 0.0)
        idx = jnp.where(sub == k, picks[k], idx)
        gate = jnp.where(sub == k, weights[k] / wsum * ROUTED_SCALE, gate)
        rank = jnp.where(sub == k, jnp.sum(r, axis=0, keepdims=True), rank)
    idx_ref[...] = idx
    gate_ref[...] = gate
    rank_ref[...] = rank.astype(jnp.int32)


def _route(logits_t, router_bias):
    m = logits_t.shape[1]
    tile = ROUTE_TILE
    j = lax.broadcasted_iota(jnp.int32, (tile, tile), 0)
    t = lax.broadcasted_iota(jnp.int32, (tile, tile), 1)
    uo = jnp.concatenate([(j < t).astype(BF16), jnp.ones((tile, tile), BF16)], axis=1)
    bias = jnp.broadcast_to(router_bias[:, None], (N_EXPERTS, tile))
    col = lambda i: (0, i)
    const = lambda i: (0, 0)
    idx, gate, rank, cnt = pl.pallas_call(
        _route_kernel,
        out_shape=(jax.ShapeDtypeStruct((TOP_K, m), jnp.int32),
                   jax.ShapeDtypeStruct((TOP_K, m), F32),
                   jax.ShapeDtypeStruct((TOP_K, m), jnp.int32),
                   jax.ShapeDtypeStruct((N_EXPERTS, tile), F32)),
        grid=(m // tile,),
        in_specs=[pl.BlockSpec((N_EXPERTS, tile), col),
                  pl.BlockSpec((N_EXPERTS, tile), const),
                  pl.BlockSpec((tile, 2 * tile), const)],
        out_specs=(pl.BlockSpec((TOP_K, tile), col),
                   pl.BlockSpec((TOP_K, tile), col),
                   pl.BlockSpec((TOP_K, tile), col),
                   pl.BlockSpec((N_EXPERTS, tile), const)),
        compiler_params=_params("arbitrary"),
        name="moe_route",
    )(logits_t, bias, uo)
    return idx, gate, rank, cnt[:, 0].astype(jnp.int32)


def _dispatch(idx, rank, counts):
    m = idx.shape[1]
    r = MOE_ROWS
    padded = (counts + r - 1) // r * r
    pend = jnp.cumsum(padded)
    pstart = pend - padded
    experts = jnp.arange(N_EXPERTS, dtype=jnp.int32)
    dest = jnp.sum(jnp.where(idx[:, :, None] == experts, pstart.astype(jnp.int32), 0), axis=-1) + rank
    nb = -(-(m * TOP_K) // r) + N_EXPERTS
    blk_lo = jnp.arange(nb, dtype=jnp.int32) * r
    blk_e = jnp.minimum(jnp.sum(pend[None, :] <= blk_lo[:, None], axis=1), N_EXPERTS - 1).astype(jnp.int32)
    nvalid = jnp.clip((pstart + counts)[blk_e] - blk_lo, 0, r).astype(jnp.int32)
    pad_blk = jnp.concatenate([jnp.maximum(pend // r - 1, 0), pend[-1:] // r]).astype(jnp.int32)
    return dest, blk_e, nvalid, pad_blk


def _scatter_kernel(last_blk_ref, dest_ref, xa_ref, xb_ref, xg_ref, zero_ref, sem, zsem, *, tile, steps_a):
    i = pl.program_id(0)

    @pl.when(i == 0)
    def _():
        zero_ref[...] = jnp.zeros_like(zero_ref)

        def zero_copy(blk):
            start = pl.multiple_of(blk * MOE_ROWS, MOE_ROWS)
            return pltpu.make_async_copy(zero_ref, xg_ref.at[pl.ds(start, MOE_ROWS), :], zsem)

        def start(e, carry):
            zero_copy(last_blk_ref[e]).start()
            return carry

        def wait(e, carry):
            zero_copy(last_blk_ref[e]).wait()
            return carry

        lax.fori_loop(0, N_EXPERTS, start, 0)
        lax.fori_loop(0, N_EXPERTS, wait, 0)

        def start_tail(blk, carry):
            zero_copy(blk).start()
            return carry

        def wait_tail(blk, carry):
            zero_copy(blk).wait()
            return carry

        used = last_blk_ref[N_EXPERTS]
        lax.fori_loop(used, xg_ref.shape[0] // MOE_ROWS, start_tail, 0)
        lax.fori_loop(used, xg_ref.shape[0] // MOE_ROWS, wait_tail, 0)

    def push(x_ref):
        def body(t, carry):
            for k in range(TOP_K):
                pltpu.make_async_copy(x_ref.at[pl.ds(t, 1), :], xg_ref.at[pl.ds(dest_ref[k, t], 1), :],
                                      sem).start(priority=k % 2)
            return carry
        lax.fori_loop(0, tile, body, 0)
        for k in range(TOP_K):
            pltpu.make_async_copy(x_ref, xg_ref.at[pl.ds(0, tile), :], sem).wait()

    @pl.when(i < steps_a)
    def _():
        push(xa_ref)

    @pl.when(i >= steps_a)
    def _():
        push(xb_ref)


def _moe_scatter(last_blk, dest, x_a, x_b, cap):
    tile = ROW_TILE
    steps_a, steps_b = x_a.shape[0] // tile, x_b.shape[0] // tile
    return pl.pallas_call(
        functools.partial(_scatter_kernel, tile=tile, steps_a=steps_a),
        out_shape=jax.ShapeDtypeStruct((cap, D_MODEL), F32),
        grid=(steps_a + steps_b,),
        in_specs=[pl.BlockSpec(memory_space=pltpu.SMEM),
                  pl.BlockSpec((None, TOP_K, tile), lambda i: (i, 0, 0), memory_space=pltpu.SMEM),
                  pl.BlockSpec((tile, D_MODEL), lambda i: (jnp.minimum(i, steps_a - 1), 0)),
                  pl.BlockSpec((tile, D_MODEL), lambda i: (jnp.maximum(i - steps_a, 0), 0))],
        out_specs=pl.BlockSpec(memory_space=pl.ANY),
        scratch_shapes=[pltpu.VMEM((MOE_ROWS, D_MODEL), F32), pltpu.SemaphoreType.DMA(()),
                        pltpu.SemaphoreType.DMA(())],
        compiler_params=_params("arbitrary"),
        name="moe_scatter",
    )(last_blk, dest, x_a, x_b)


def _moe_kernel(blk_e_ref, nvalid_ref, x_ref, wg_ref, wu_ref, wd_ref, o_ref, wgb, wub, wdb):
    i = pl.program_id(0)
    e = blk_e_ref[i]
    prev = blk_e_ref[jnp.maximum(i - 1, 0)]

    @pl.when((i == 0) | (e != prev))
    def _():
        wgb[...] = wg_ref[...].astype(BF16)
        wub[...] = wu_ref[...].astype(BF16)
        wdb[...] = wd_ref[...].astype(BF16)

    @pl.when(nvalid_ref[i] > 0)
    def _():
        x = x_ref[...].astype(BF16)
        act = (_silu(_dot(x, wgb[...])) * _dot(x, wub[...])).astype(BF16)
        o_ref[...] = _dot(act, wdb[...])

    @pl.when(nvalid_ref[i] == 0)
    def _():
        o_ref[...] = jnp.zeros_like(o_ref)


def _moe_experts(blk_e, nvalid, xg, w_gate_e, w_up_e, w_down_e):
    cap = xg.shape[0]
    r = MOE_ROWS
    grid_spec = pltpu.PrefetchScalarGridSpec(
        num_scalar_prefetch=2,
        grid=(cap // r,),
        in_specs=[pl.BlockSpec((r, D_MODEL), lambda i, be, nv: (i, 0)),
                  pl.BlockSpec((None, D_MODEL, D_EXPERT), lambda i, be, nv: (be[i], 0, 0)),
                  pl.BlockSpec((None, D_MODEL, D_EXPERT), lambda i, be, nv: (be[i], 0, 0)),
                  pl.BlockSpec((None, D_EXPERT, D_MODEL), lambda i, be, nv: (be[i], 0, 0))],
        out_specs=pl.BlockSpec((r, D_MODEL), lambda i, be, nv: (i, 0)),
        scratch_shapes=[pltpu.VMEM((D_MODEL, D_EXPERT), BF16),
                        pltpu.VMEM((D_MODEL, D_EXPERT), BF16),
                        pltpu.VMEM((D_EXPERT, D_MODEL), BF16)])
    return pl.pallas_call(
        _moe_kernel,
        out_shape=jax.ShapeDtypeStruct((cap, D_MODEL), F32),
        grid_spec=grid_spec,
        compiler_params=_params("arbitrary"),
        name="moe_experts",
    )(blk_e, nvalid, xg, w_gate_e, w_up_e, w_down_e)


def _final_kernel(dcur_ref, dnext_ref, x1_ref, gate_ref, rows_ref, wg_ref, wu_ref, wd_ref, g_ref, b_ref,
                  o_ref, buf, sem, *, tile):
    i = pl.program_id(0)
    n = pl.num_programs(0)
    slot = i % 2

    def gather(dest_ref, s):
        def body(t, carry):
            for k in range(TOP_K):
                pltpu.make_async_copy(rows_ref.at[pl.ds(dest_ref[k, t], 1), :],
                                      buf.at[s, k, pl.ds(t, 1), :], sem.at[s]).start(priority=k % 2)
            return carry
        lax.fori_loop(0, tile, body, 0)

    @pl.when(i == 0)
    def _():
        gather(dcur_ref, 0)

    @pl.when(i + 1 < n)
    def _():
        gather(dnext_ref, 1 - slot)

    x1 = x1_ref[...]
    xb = x1.astype(BF16)
    act = (_silu(_dot(xb, wg_ref[...])) * _dot(xb, wu_ref[...])).astype(BF16)
    moe = _dot(act, wd_ref[...])
    for k in range(TOP_K):
        pltpu.make_async_copy(rows_ref.at[pl.ds(0, tile), :], buf.at[slot, k], sem.at[slot]).wait()
    gate = gate_ref[...]
    for k in range(TOP_K):
        moe = moe + buf[slot, k] * gate[:, k:k + 1]
    o_ref[...] = _layer_norm(ALPHA * x1 + moe, g_ref[...], b_ref[...])


def _final(x1, dest, gates_t, rows, row_off, w_gate_s, w_up_s, w_down_s, ln_g, ln_b):
    m = x1.shape[0]
    tile = ROW_TILE
    n = m // tile
    off = row_off // tile
    row = lambda i: (i, 0)
    const = lambda i: (0, 0)
    return pl.pallas_call(
        functools.partial(_final_kernel, tile=tile),
        out_shape=jax.ShapeDtypeStruct((m, D_MODEL), F32),
        grid=(n,),
        in_specs=[pl.BlockSpec((None, TOP_K, tile), lambda i: (off + i, 0, 0), memory_space=pltpu.SMEM),
                  pl.BlockSpec((None, TOP_K, tile), lambda i: (off + jnp.minimum(i + 1, n - 1), 0, 0),
                               memory_space=pltpu.SMEM),
                  pl.BlockSpec((tile, D_MODEL), row),
                  pl.BlockSpec((tile, TOP_K), lambda i: (off + i, 0)),
                  pl.BlockSpec(memory_space=pl.ANY),
                  pl.BlockSpec((D_MODEL, D_EXPERT), const),
                  pl.BlockSpec((D_MODEL, D_EXPERT), const),
                  pl.BlockSpec((D_EXPERT, D_MODEL), const),
                  pl.BlockSpec((1, D_MODEL), const),
                  pl.BlockSpec((1, D_MODEL), const)],
        out_specs=pl.BlockSpec((tile, D_MODEL), row),
        scratch_shapes=[pltpu.VMEM((2, TOP_K, tile, D_MODEL), F32), pltpu.SemaphoreType.DMA((2,))],
        compiler_params=_params("arbitrary"),
        name="shared_ln",
    )(dest, dest, x1, gates_t, rows, w_gate_s, w_up_s, w_down_s, ln_g[None, :], ln_b[None, :])


def kernel(x_prompt, x_sample, cache_k, cache_v, state_conv, state_ssm, page_table, w_in, sb_bias, conv_w, conv_b, dt_bias, a_log, d_skip, g_attn, g_ssd, w_out, ln1_g, ln1_b, w_router, router_bias, w_gate_e, w_up_e, w_down_e, w_gate_s, w_up_s, w_down_s, ln2_g, ln2_b):
    assert w_in.shape[0] == 1, "one layer"
    bp, seq, _ = x_prompt.shape
    bs, dec, _ = x_sample.shape
    assert dec >= CONV_WIDTH - 1
    mp, ms = bp * seq, bs * dec
    n_phys = cache_k.shape[1]

    w_in_b = jnp.pad(w_in[0], ((0, 0), (0, IN_PAD - w_in.shape[2]))).astype(BF16)
    w_out_b = w_out[0].astype(BF16)
    w_router_t = w_router[0].T.astype(BF16)
    mix_w = (g_attn[0], g_ssd[0], w_out_b, ln1_g[0], ln1_b[0], w_router_t)
    ssd_w = (conv_w[0], conv_b[0], dt_bias[0], a_log[0], d_skip[0])
    dt_blk = IN_MAIN // LANES
    xbc_blk = (3 * ATTN_WIDTH + SSM_WIDTH) // CONV_DIM

    xp = x_prompt.reshape(mp, D_MODEL)
    proj_p = _proj(xp, w_in_b, 1024, 896)
    proj_p3 = proj_p.reshape(bp, seq, IN_PAD)
    k_prompt = proj_p3[:, :, ATTN_WIDTH:2 * ATTN_WIDTH].reshape(1, bp, seq, N_HEADS, HEAD_DIM)
    v_prompt = proj_p3[:, :, 2 * ATTN_WIDTH:3 * ATTN_WIDTH].reshape(1, bp, seq, N_HEADS, HEAD_DIM)
    conv_prompt = proj_p3[:, seq - (CONV_WIDTH - 1):, IN_MAIN - CONV_DIM:IN_MAIN][None]
    oa_p = _attn_prompt(proj_p, sb_bias[0], bp, seq)
    dtr_p = proj_p3[:, :, IN_MAIN:IN_MAIN + N_SSM_HEADS].transpose(0, 2, 1)
    y_p, h_p = _ssd(proj_p3, xbc_blk, jnp.zeros((bp, SUBLANES, CONV_DIM), F32),
                    jnp.zeros((bp, N_SSM_HEADS, SSM_HEAD_DIM, SSM_STATE), F32),
                    proj_p3, dt_blk, dtr_p, *ssd_w, rows_in=SSD_CHUNK, valid=SSD_CHUNK)
    x1_p, lg_p = _mix_out(oa_p, y_p.reshape(mp, SSM_WIDTH), proj_p, xp, *mix_w, tm=256)

    xs = x_sample.reshape(ms, D_MODEL)
    proj_s = _proj(xs, w_in_b, ms, 896)
    proj_s3 = proj_s.reshape(bs, dec, IN_PAD)
    k_sample = proj_s3[:, :, ATTN_WIDTH:2 * ATTN_WIDTH].reshape(1, bs, dec, N_HEADS, HEAD_DIM)
    v_sample = proj_s3[:, :, 2 * ATTN_WIDTH:3 * ATTN_WIDTH].reshape(1, bs, dec, N_HEADS, HEAD_DIM)
    conv_sample = proj_s3[:, dec - (CONV_WIDTH - 1):, IN_MAIN - CONV_DIM:IN_MAIN][None]
    proj_s8 = jnp.pad(proj_s3, ((0, 0), (0, SUBLANES - dec), (0, 0)))
    oa_s = _attn_sample(proj_s8, cache_k.reshape(n_phys, PAGE_SIZE * N_HEADS, HEAD_DIM),
                        cache_v.reshape(n_phys, PAGE_SIZE * N_HEADS, HEAD_DIM), page_table, sb_bias[0])
    dtc_s = jnp.pad(proj_s8[:, :, IN_MAIN:], ((0, 0), (0, SSD_CHUNK - SUBLANES), (0, 0)))
    dtr_s = dtc_s[:, :, :N_SSM_HEADS].transpose(0, 2, 1)
    conv0_s = jnp.pad(state_conv[0], ((0, 0), (SUBLANES - (CONV_WIDTH - 1), 0), (0, 0)))
    y_s, h_s = _ssd(proj_s8, xbc_blk, conv0_s, state_ssm[0], dtc_s, 0, dtr_s, *ssd_w,
                    rows_in=SUBLANES, valid=dec)
    x1_s, lg_s = _mix_out(oa_s[:, :dec].reshape(ms, ATTN_WIDTH), y_s[:, :dec].reshape(ms, SSM_WIDTH),
                                 proj_s, xs, *mix_w, tm=ms)

    m = mp + ms
    idx, gates, rank, counts = _route(jnp.concatenate([lg_p, lg_s], axis=1), router_bias[0])
    dest, blk_e, nvalid, pad_blk = _dispatch(idx, rank, counts)
    dest_t = dest.reshape(TOP_K, m // ROW_TILE, ROW_TILE).transpose(1, 0, 2)
    xg = _moe_scatter(pad_blk, dest_t, x1_p, x1_s, blk_e.shape[0] * MOE_ROWS)
    out_rows = _moe_experts(blk_e, nvalid, xg, w_gate_e[0], w_up_e[0], w_down_e[0])
    shared_w = (w_gate_s[0].astype(BF16), w_up_s[0].astype(BF16), w_down_s[0].astype(BF16), ln2_g[0], ln2_b[0])
    y_p = _final(x1_p, dest_t, gates.T, out_rows, 0, *shared_w)
    y_s = _final(x1_s, dest_t, gates.T, out_rows, mp, *shared_w)

    return (y_p.reshape(bp, seq, D_MODEL), y_s.reshape(bs, dec, D_MODEL),
            k_prompt, v_prompt, conv_prompt, h_p[None],
            k_sample, v_sample, conv_sample, h_s[None])
```

```python
import functools

import jax
import jax.numpy as jnp
from jax import lax
from jax.experimental import pallas as pl
from jax.experimental.pallas import tpu as pltpu

F32 = jnp.float32
BF16 = jnp.bfloat16

D_MODEL = 2048
ATTN_WIDTH = 1024
SSM_WIDTH = 1024
HEAD_DIM = 128
N_HEADS = 8
SB_SCALE = HEAD_DIM ** -0.5
SSM_HEAD_DIM = 64
N_SSM_HEADS = 16
SSM_STATE = 128
N_GROUPS = 4
HEADS_PER_GROUP = N_SSM_HEADS // N_GROUPS
CONV_WIDTH = 4
CONV_DIM = SSM_WIDTH + 2 * N_GROUPS * SSM_STATE
IN_MAIN = 3 * ATTN_WIDTH + SSM_WIDTH + CONV_DIM
IN_PAD = IN_MAIN + 128
N_EXPERTS = 64
N_EXPERT_GROUPS = 8
EXPERTS_PER_GROUP = N_EXPERTS // N_EXPERT_GROUPS
TOPK_GROUPS = 4
TOP_K = 8
D_EXPERT = 512
ROUTED_SCALE = 2.5
ALPHA = 2.0 ** 0.25
LN_EPS = 1e-5
PAGE_SIZE = 128

LANES = 128
SUBLANES = 8
VMEM_LIMIT = 56 * 1024 * 1024

SSD_CHUNK = 128
ATTN_BLOCK = 256
ATTN_HEADS_PER_STEP = 4
PAGES_PER_STEP = 4
PAGE_RING = 3
MOE_ROWS = 512
ROUTE_TILE = 128
ROW_TILE = 128
NEG_BIG = -1e30
LOG2E = 1.4426950408889634


def _params(*sem):
    return pltpu.CompilerParams(dimension_semantics=sem, vmem_limit_bytes=VMEM_LIMIT)


def _softplus(x):
    return jnp.maximum(x, 0.0) + jnp.log1p(jnp.exp(-jnp.abs(x)))


def _silu(x):
    return x * (1.0 / (1.0 + jnp.exp(-x)))


def _dot(a, b, **kw):
    return jnp.dot(a, b, preferred_element_type=F32, **kw)


def _dot_nt(a, b):
    return lax.dot_general(a, b, (((1,), (1,)), ((), ())), preferred_element_type=F32)


def _dot_tn(a, b):
    return lax.dot_general(a, b, (((0,), (0,)), ((), ())), preferred_element_type=F32)


def _proj_kernel(x_ref, w_ref, o_ref):
    o_ref[...] = _dot(x_ref[...].astype(BF16), w_ref[...])


def _proj(x, w, tm, tn):
    m, d = x.shape
    n = w.shape[1]
    return pl.pallas_call(
        _proj_kernel,
        out_shape=jax.ShapeDtypeStruct((m, n), F32),
        grid=(m // tm, n // tn),
        in_specs=[pl.BlockSpec((tm, d), lambda i, j: (i, 0)),
                  pl.BlockSpec((d, tn), lambda i, j: (0, j))],
        out_specs=pl.BlockSpec((tm, tn), lambda i, j: (i, j)),
        compiler_params=_params("parallel", "arbitrary"),
        name="in_proj",
    )(x, w)


def _neg_suffix(n):
    j = lax.broadcasted_iota(jnp.int32, (n, n), 0)
    s = lax.broadcasted_iota(jnp.int32, (n, n), 1)
    return -jnp.concatenate([(j > s).astype(BF16), jnp.ones((n, LANES), BF16)], axis=1)


def _sb_prep(z, valid, uo):
    sp = jnp.maximum(z, 0.0) + jnp.log(1.0 + jnp.exp2(jnp.abs(z) * -LOG2E))
    spm = sp if valid is None else jnp.where(valid, sp, 0.0)
    return z - sp, _dot(spm.astype(BF16), uo)


def _sb_weights(log_beta, lc, valid, c, nk):
    cb = c if nk == LANES else jnp.concatenate([c] * (nk // LANES), axis=1)
    a = jnp.exp2((log_beta + lc[:, :nk] + cb) * LOG2E)
    if valid is not None:
        a = jnp.where(valid, a, 0.0)
    return a, c + lc[:, nk:]


def _sb_block(z, valid, c, uo, nk):
    log_beta, lc = _sb_prep(z, valid, uo)
    return _sb_weights(log_beta, lc, valid, c, nk)


def _attn_prompt_kernel(bias_ref, q_ref, k_ref, v_ref, uo_ref, o_ref, kb_ref, vb_ref, *, blk, heads):
    hg = pl.program_id(1)
    i = pl.program_id(2)

    @pl.when(i == 0)
    def _():
        kb_ref[...] = k_ref[...].astype(BF16)
        vb_ref[...] = v_ref[...].astype(BF16)

    uo = uo_ref[...]
    row = lax.broadcasted_iota(jnp.int32, (blk, blk), 0)
    col = lax.broadcasted_iota(jnp.int32, (blk, blk), 1)
    qs = [q_ref[:, h * HEAD_DIM:(h + 1) * HEAD_DIM].astype(BF16) for h in range(heads)]
    biases = [bias_ref[hg * heads + h] for h in range(heads)]

    def block(j, carry, valid):
        start = pl.multiple_of(j * blk, blk)
        out = []
        for h in range(heads):
            c, acc = carry[h]
            lanes = slice(h * HEAD_DIM, (h + 1) * HEAD_DIM)
            z = _dot_nt(qs[h], kb_ref[pl.ds(start, blk), lanes]) * SB_SCALE + biases[h]
            a, c = _sb_block(z, valid, c, uo, blk)
            out.append((c, acc + _dot(a.astype(BF16), vb_ref[pl.ds(start, blk), lanes])))
        return tuple(out)

    zero = jnp.zeros((blk, HEAD_DIM), F32)
    carry = block(i, ((zero, zero),) * heads, col < row)
    carry = lax.fori_loop(0, i, lambda jj, cr: block(i - 1 - jj, cr, None), carry)
    for h in range(heads):
        o_ref[:, h * HEAD_DIM:(h + 1) * HEAD_DIM] = carry[h][1]


def _attn_prompt(proj, sb_bias, batch, seq):
    blk = ATTN_BLOCK
    heads = ATTN_HEADS_PER_STEP
    nq = seq // blk
    ng = N_HEADS // heads
    width = heads * HEAD_DIM
    return pl.pallas_call(
        functools.partial(_attn_prompt_kernel, blk=blk, heads=heads),
        out_shape=jax.ShapeDtypeStruct((batch * seq, ATTN_WIDTH), F32),
        grid=(batch, ng, nq),
        in_specs=[pl.BlockSpec(memory_space=pltpu.SMEM),
                  pl.BlockSpec((blk, width), lambda b, g, i: (b * nq + i, g)),
                  pl.BlockSpec((seq, width), lambda b, g, i: (b, ng + g)),
                  pl.BlockSpec((seq, width), lambda b, g, i: (b, 2 * ng + g)),
                  pl.BlockSpec((blk, blk + LANES), lambda b, g, i: (0, 0))],
        out_specs=pl.BlockSpec((blk, width), lambda b, g, i: (b * nq + i, g)),
        scratch_shapes=[pltpu.VMEM((seq, width), BF16), pltpu.VMEM((seq, width), BF16)],
        compiler_params=_params("parallel", "parallel", "arbitrary"),
        name="sb_attn_prompt",
    )(sb_bias, proj, proj, proj, _neg_suffix(blk))


def _attn_sample_kernel(pt_ref, bias_ref, q_ref, kn_ref, vn_ref, ck_ref, cv_ref, uo_ref, o_ref,
                        kbuf, vbuf, sem, *, pps, nbuf, n_pages):
    b = pl.program_id(0)
    steps = n_pages // pps
    rows = N_HEADS * SUBLANES
    uo = uo_ref[...]
    qb = q_ref[...].astype(BF16)
    rh = lax.broadcasted_iota(jnp.int32, (rows, PAGE_SIZE), 0) // SUBLANES
    bias_v = jnp.zeros((rows, PAGE_SIZE), F32)
    for h in range(N_HEADS):
        bias_v = jnp.where(rh == h, bias_ref[h], bias_v)

    def page_copies(step, slot):
        out = []
        for u in range(pps):
            page = pt_ref[b, n_pages - 1 - (step * pps + u)]
            out.append(pltpu.make_async_copy(ck_ref.at[page], kbuf.at[slot, u], sem.at[slot]))
            out.append(pltpu.make_async_copy(cv_ref.at[page], vbuf.at[slot, u], sem.at[slot]))
        return out

    def sweep(blocks, valid, c, acc):
        zs = [jnp.concatenate([_dot_nt(qb[:, h * HEAD_DIM:(h + 1) * HEAD_DIM], load_k(h))
                               for h in range(N_HEADS)], axis=0) * SB_SCALE + bias_v
              for load_k, _ in blocks]
        preps = [_sb_prep(z, valid, uo) for z in zs]
        weights = []
        for log_beta, lc in preps:
            a, c = _sb_weights(log_beta, lc, valid, c, PAGE_SIZE)
            weights.append(a)
        for a, (_, load_v) in zip(weights, blocks):
            acc = acc + jnp.concatenate(
                [_dot(a[h * SUBLANES:(h + 1) * SUBLANES].astype(BF16), load_v(h)) for h in range(N_HEADS)], axis=0)
        return c, acc

    for step in range(nbuf - 1):
        for cp in page_copies(step, step):
            cp.start()

    pad = jnp.zeros((PAGE_SIZE - SUBLANES, HEAD_DIM), F32)

    def new_rows(ref):
        return lambda h: jnp.concatenate([ref[:, h * HEAD_DIM:(h + 1) * HEAD_DIM], pad], axis=0).astype(BF16)

    t = lax.broadcasted_iota(jnp.int32, (rows, PAGE_SIZE), 0) % SUBLANES
    key = lax.broadcasted_iota(jnp.int32, (rows, PAGE_SIZE), 1)
    carry = sweep([(new_rows(kn_ref), new_rows(vn_ref))], key < t,
                  jnp.zeros((rows, LANES), F32), jnp.zeros((rows, HEAD_DIM), F32))

    def page_rows(buf, slot, u):
        return lambda h: buf[slot, u, pl.ds(h, PAGE_SIZE, stride=N_HEADS), :].astype(BF16)

    def body(step, carry):
        slot = step % nbuf
        ahead = step + (nbuf - 1)

        @pl.when(ahead < steps)
        def _():
            for cp in page_copies(ahead, ahead % nbuf):
                cp.start()

        for cp in page_copies(step, slot):
            cp.wait()
        return sweep([(page_rows(kbuf, slot, u), page_rows(vbuf, slot, u)) for u in range(pps)], None, *carry)

    _, acc = lax.fori_loop(0, steps, body, carry)
    for h in range(N_HEADS):
        o_ref[:, h * HEAD_DIM:(h + 1) * HEAD_DIM] = acc[h * SUBLANES:(h + 1) * SUBLANES, :]


def _attn_sample(proj8, cache_k, cache_v, page_table, sb_bias):
    batch = proj8.shape[0]
    n_pages = page_table.shape[1]
    pps, nbuf = PAGES_PER_STEP, PAGE_RING
    assert n_pages % pps == 0 and n_pages // pps >= nbuf

    def tok_spec(col):
        return pl.BlockSpec((None, SUBLANES, ATTN_WIDTH), lambda b, pt: (b, 0, col))

    grid_spec = pltpu.PrefetchScalarGridSpec(
        num_scalar_prefetch=1,
        grid=(batch,),
        in_specs=[pl.BlockSpec(memory_space=pltpu.SMEM), tok_spec(0), tok_spec(1), tok_spec(2),
                  pl.BlockSpec(memory_space=pl.ANY), pl.BlockSpec(memory_space=pl.ANY),
                  pl.BlockSpec((PAGE_SIZE, 2 * LANES), lambda b, pt: (0, 0))],
        out_specs=pl.BlockSpec((None, SUBLANES, ATTN_WIDTH), lambda b, pt: (b, 0, 0)),
        scratch_shapes=[pltpu.VMEM((nbuf, pps, PAGE_SIZE * N_HEADS, HEAD_DIM), F32),
                        pltpu.VMEM((nbuf, pps, PAGE_SIZE * N_HEADS, HEAD_DIM), F32),
                        pltpu.SemaphoreType.DMA((nbuf,))])
    return pl.pallas_call(
        functools.partial(_attn_sample_kernel, pps=pps, nbuf=nbuf, n_pages=n_pages),
        out_shape=jax.ShapeDtypeStruct((batch, SUBLANES, ATTN_WIDTH), F32),
        grid_spec=grid_spec,
        compiler_params=_params("arbitrary"),
        name="sb_attn_sample",
    )(page_table, sb_bias, proj8, proj8, proj8, cache_k, cache_v, _neg_suffix(PAGE_SIZE))


def _ssd_kernel(x_ref, conv0_ref, h0_ref, dtc_ref, dtr_ref, cw_ref, cb_ref,
                dtb_c_ref, alog_c_ref, dtb_r_ref, alog_r_ref, dskip_ref, y_ref, hout_ref, ext_ref,
                *, rows_in, valid):
    q = SSD_CHUNK
    c = pl.program_id(1)

    @pl.when(c == 0)
    def _():
        ext_ref[0:SUBLANES, :] = conv0_ref[...]
        hout_ref[...] = h0_ref[...]

    if rows_in < q:
        ext_ref[SUBLANES:, :] = jnp.zeros((q, CONV_DIM), F32)
    ext_ref[SUBLANES:SUBLANES + rows_in, :] = x_ref[...]
    cw = cw_ref[...]
    xc = cb_ref[...] + ext_ref[5:5 + q, :] * cw[0:1, :]
    xc = xc + ext_ref[6:6 + q, :] * cw[1:2, :]
    xc = xc + ext_ref[7:7 + q, :] * cw[2:3, :]
    xc = _silu(xc + ext_ref[8:8 + q, :] * cw[3:4, :])
    ext_ref[0:SUBLANES, :] = ext_ref[q:q + SUBLANES, :]

    t_c = lax.broadcasted_iota(jnp.int32, (q, LANES), 0)
    h_c = lax.broadcasted_iota(jnp.int32, (q, LANES), 1)
    dt_c = jnp.where((t_c < valid) & (h_c < N_SSM_HEADS), _softplus(dtc_ref[...] + dtb_c_ref[...]), 0.0)
    t_r = lax.broadcasted_iota(jnp.int32, (N_SSM_HEADS, q), 1)
    dt_r = jnp.where(t_r < valid, _softplus(dtr_ref[...] + dtb_r_ref[...]), 0.0)
    ti = lax.broadcasted_iota(jnp.int32, (q, q), 0)
    tj = lax.broadcasted_iota(jnp.int32, (q, q), 1)
    causal = tj <= ti
    acum_c = _dot(causal.astype(F32), dt_c * -jnp.exp(alog_c_ref[...]), precision=lax.Precision.HIGHEST)
    acum_r = _dot(dt_r * -jnp.exp(alog_r_ref[...]), (ti <= tj).astype(F32), precision=lax.Precision.HIGHEST)

    first = lax.broadcasted_iota(jnp.int32, (q, LANES), 1) < SSM_HEAD_DIM
    first_row = lax.broadcasted_iota(jnp.int32, (2 * SSM_HEAD_DIM, SSM_STATE), 0) < SSM_HEAD_DIM
    for g in range(N_GROUPS):
        lo = SSM_WIDTH + g * SSM_STATE
        bg = xc[:, lo:lo + SSM_STATE].astype(BF16)
        lo = SSM_WIDTH + (N_GROUPS + g) * SSM_STATE
        cg = xc[:, lo:lo + SSM_STATE].astype(BF16)
        cb = _dot_nt(cg, bg)
        for pair in range(g * HEADS_PER_GROUP // 2, (g + 1) * HEADS_PER_GROUP // 2):
            heads = (2 * pair, 2 * pair + 1)
            ac = [jnp.broadcast_to(acum_c[:, h:h + 1], (q, LANES)) for h in heads]
            dt = [jnp.broadcast_to(dt_c[:, h:h + 1], (q, LANES)) for h in heads]
            m = [(cb * jnp.exp(jnp.where(causal, ac[k] - acum_r[h:h + 1, :], NEG_BIG))).astype(BF16)
                 for k, h in enumerate(heads)]
            acp = jnp.where(first, ac[0], ac[1])
            dtp = jnp.where(first, dt[0], dt[1])
            lanes = slice(pair * LANES, (pair + 1) * LANES)
            xs = xc[:, lanes]
            xdt = xs * dtp
            y = _dot(m[0], jnp.where(first, xdt, 0.0).astype(BF16))
            y = y + _dot(m[1], jnp.where(first, 0.0, xdt).astype(BF16))
            hst = hout_ref[pair]
            y = y + _dot_nt(cg, hst.astype(BF16)) * jnp.exp(acp)
            y = y + dskip_ref[:, lanes] * xs
            xw = (xs * (jnp.exp(acp[q - 1:q, :] - acp) * dtp)).astype(BF16)
            keep = jnp.where(first_row, jnp.exp(acum_r[heads[0]:heads[0] + 1, q - 1:q]),
                             jnp.exp(acum_r[heads[1]:heads[1] + 1, q - 1:q]))
            hout_ref[pair] = hst * keep + _dot_tn(xw, bg)
            y_ref[:, lanes] = y[:rows_in]


def _ssd(x3, col_blk, conv0, h0, dtc, dtc_blk, dtr, conv_w, conv_b, dt_bias, a_log, d_skip, rows_in, valid):
    batch, length = x3.shape[0], x3.shape[1]
    nc = max(length // SSD_CHUNK, 1)
    lane_pad = (0, LANES - N_SSM_HEADS)
    dtb_c = jnp.pad(dt_bias, lane_pad)[None, :]
    alog_c = jnp.pad(a_log, lane_pad)[None, :]
    dtb_r = jnp.broadcast_to(dt_bias[:, None], (N_SSM_HEADS, LANES))
    alog_r = jnp.broadcast_to(a_log[:, None], (N_SSM_HEADS, LANES))
    dskip = jnp.repeat(d_skip, SSM_HEAD_DIM)[None, :]
    const2 = lambda b, c: (0, 0)
    pairs = N_SSM_HEADS // 2
    state_shape = (batch, pairs, 2 * SSM_HEAD_DIM, SSM_STATE)
    state_spec = pl.BlockSpec((None, pairs, 2 * SSM_HEAD_DIM, SSM_STATE), lambda b, c: (b, 0, 0, 0))
    y, h = pl.pallas_call(
        functools.partial(_ssd_kernel, rows_in=rows_in, valid=valid),
        out_shape=(jax.ShapeDtypeStruct((batch, length, SSM_WIDTH), F32),
                   jax.ShapeDtypeStruct(state_shape, F32)),
        grid=(batch, nc),
        in_specs=[pl.BlockSpec((None, rows_in, CONV_DIM), lambda b, c: (b, c, col_blk)),
                  pl.BlockSpec((None, SUBLANES, CONV_DIM), lambda b, c: (b, 0, 0)),
                  state_spec,
                  pl.BlockSpec((None, SSD_CHUNK, LANES), lambda b, c: (b, c, dtc_blk)),
                  pl.BlockSpec((None, N_SSM_HEADS, SSD_CHUNK), lambda b, c: (b, 0, c)),
                  pl.BlockSpec((CONV_WIDTH, CONV_DIM), const2),
                  pl.BlockSpec((1, CONV_DIM), const2),
                  pl.BlockSpec((1, LANES), const2),
                  pl.BlockSpec((1, LANES), const2),
                  pl.BlockSpec((N_SSM_HEADS, LANES), const2),
                  pl.BlockSpec((N_SSM_HEADS, LANES), const2),
                  pl.BlockSpec((1, SSM_WIDTH), const2)],
        out_specs=(pl.BlockSpec((None, rows_in, SSM_WIDTH), lambda b, c: (b, c, 0)), state_spec),
        scratch_shapes=[pltpu.VMEM((SSD_CHUNK + SUBLANES, CONV_DIM), F32)],
        compiler_params=_params("parallel", "arbitrary"),
        name="ssd_mixer",
    )(x3, conv0, h0.reshape(state_shape), dtc, dtr, conv_w, conv_b[None, :], dtb_c, alog_c, dtb_r, alog_r, dskip)
    return y, h.reshape(batch, N_SSM_HEADS, SSM_HEAD_DIM, SSM_STATE)


def _layer_norm(u, g, b):
    mu = jnp.mean(u, axis=-1, keepdims=True)
    d = u - mu
    var = jnp.mean(d * d, axis=-1, keepdims=True)
    return d * lax.rsqrt(var + LN_EPS) * g + b


def _mix_out_kernel(oa_ref, ys_ref, z_ref, x_ref, ga_ref, gs_ref, wo_ref, g1_ref, b1_ref, wr_ref,
                    x1_ref, lg_ref):
    oa = oa_ref[...]
    na = oa * lax.rsqrt(jnp.mean(oa * oa, axis=-1, keepdims=True) + LN_EPS) * ga_ref[...]
    m = _dot(na.astype(BF16), wo_ref[0:ATTN_WIDTH, :])
    yz = ys_ref[...] * _silu(z_ref[...])
    gw = SSM_WIDTH // N_GROUPS
    gs = gs_ref[...]
    for g in range(N_GROUPS):
        yg = yz[:, g * gw:(g + 1) * gw]
        ng = yg * lax.rsqrt(jnp.mean(yg * yg, axis=-1, keepdims=True) + LN_EPS) * gs[:, g * gw:(g + 1) * gw]
        lo = ATTN_WIDTH + g * gw
        m = m + _dot(ng.astype(BF16), wo_ref[lo:lo + gw, :])
    x1 = _layer_norm(ALPHA * x_ref[...] + m, g1_ref[...], b1_ref[...])
    x1_ref[...] = x1
    lg_ref[...] = _dot_nt(wr_ref[...], x1.astype(BF16))


def _mix_out(o_attn, y_ssd, proj, x, g_attn, g_ssd, w_out, ln_g, ln_b, w_router_t, tm):
    m = x.shape[0]
    row = lambda i: (i, 0)
    const = lambda i: (0, 0)
    return pl.pallas_call(
        _mix_out_kernel,
        out_shape=(jax.ShapeDtypeStruct((m, D_MODEL), F32),
                   jax.ShapeDtypeStruct((N_EXPERTS, m), F32)),
        grid=(m // tm,),
        in_specs=[pl.BlockSpec((tm, ATTN_WIDTH), row),
                  pl.BlockSpec((tm, SSM_WIDTH), row),
                  pl.BlockSpec((tm, SSM_WIDTH), lambda i: (i, 3)),
                  pl.BlockSpec((tm, D_MODEL), row),
                  pl.BlockSpec((1, ATTN_WIDTH), const),
                  pl.BlockSpec((1, SSM_WIDTH), const),
                  pl.BlockSpec((D_MODEL, D_MODEL), const),
                  pl.BlockSpec((1, D_MODEL), const),
                  pl.BlockSpec((1, D_MODEL), const),
                  pl.BlockSpec((N_EXPERTS, D_MODEL), const)],
        out_specs=(pl.BlockSpec((tm, D_MODEL), row),
                   pl.BlockSpec((N_EXPERTS, tm), lambda i: (0, i))),
        compiler_params=_params("parallel"),
        name="mix_out_ln",
    )(o_attn, y_ssd, proj, x, g_attn[None, :], g_ssd[None, :], w_out, ln_g[None, :], ln_b[None, :], w_router_t)


def _route_kernel(lg_ref, bias_ref, uo_ref, idx_ref, gate_ref, rank_ref, cnt_ref):
    per = EXPERTS_PER_GROUP
    tile = ROUTE_TILE
    ninf = -jnp.inf

    @pl.when(pl.program_id(0) == 0)
    def _():
        cnt_ref[...] = jnp.zeros_like(cnt_ref)

    scores = 1.0 / (1.0 + jnp.exp(-lg_ref[...]))
    choice = scores + bias_ref[...]
    sub = lax.broadcasted_iota(jnp.int32, (per, tile), 0)

    def first_max(x):
        m = jnp.max(x, axis=0, keepdims=True)
        return m, jnp.min(jnp.where(x == m, sub, per), axis=0, keepdims=True)

    sc = [scores[g * per:(g + 1) * per] for g in range(N_EXPERT_GROUPS)]
    ch = [choice[g * per:(g + 1) * per] for g in range(N_EXPERT_GROUPS)]

    gscore = jnp.zeros((N_EXPERT_GROUPS, tile), F32)
    for g in range(N_EXPERT_GROUPS):
        m1, i1 = first_max(ch[g])
        m2 = jnp.max(jnp.where(sub == i1, ninf, ch[g]), axis=0, keepdims=True)
        gscore = jnp.where(sub == g, m1 + m2, gscore)
    picked = jnp.zeros((N_EXPERT_GROUPS, tile), F32)
    for _ in range(TOPK_GROUPS):
        _, ig = first_max(gscore)
        hit = sub == ig
        picked = jnp.where(hit, 1.0, picked)
        gscore = jnp.where(hit, ninf, gscore)
    mc = [jnp.where(picked[g:g + 1, :] > 0.5, ch[g], ninf) for g in range(N_EXPERT_GROUPS)]

    eidx = [sub + per * g for g in range(N_EXPERT_GROUPS)]
    onehot = [jnp.zeros((per, tile), F32)] * N_EXPERT_GROUPS
    picks, weights = [], []
    for _ in range(TOP_K):
        mx = mc[0]
        for g in range(1, N_EXPERT_GROUPS):
            mx = jnp.maximum(mx, mc[g])
        m = jnp.max(mx, axis=0, keepdims=True)
        cand = jnp.where(mc[0] == m, eidx[0], N_EXPERTS)
        for g in range(1, N_EXPERT_GROUPS):
            cand = jnp.minimum(cand, jnp.where(mc[g] == m, eidx[g], N_EXPERTS))
        ix = jnp.min(cand, axis=0, keepdims=True)
        w = jnp.zeros((per, tile), F32)
        for g in range(N_EXPERT_GROUPS):
            hit = eidx[g] == ix
            w = w + jnp.where(hit, sc[g], 0.0)
            mc[g] = jnp.where(hit, ninf, mc[g])
            onehot[g] = jnp.where(hit, 1.0, onehot[g])
        picks.append(ix)
        weights.append(jnp.sum(w, axis=0, keepdims=True))
    wsum = weights[0]
    for k in range(1, TOP_K):
        wsum = wsum + weights[k]

    lc = _dot(jnp.concatenate(onehot, axis=0).astype(BF16), uo_ref[...])
    before = lc[:, :tile] + cnt_ref[...]
    cnt_ref[...] += lc[:, tile:]
    idx = jnp.zeros((TOP_K, tile), jnp.int32)
    gate = jnp.zeros((TOP_K, tile), F32)
    rank = jnp.zeros((TOP_K, tile), F32)
    for k in range(TOP_K):
        r = jnp.zeros((per, tile), F32)
        for g in range(N_EXPERT_GROUPS):
            r = r + jnp.where(eidx[g] == picks[k], before[g * per:(g + 1) * per], 0.0)
        idx = jnp.where(sub == k, picks[k], idx)
        gate = jnp.where(sub == k, weights[k] / wsum * ROUTED_SCALE, gate)
        rank = jnp.where(sub == k, jnp.sum(r, axis=0, keepdims=True), rank)
    idx_ref[...] = idx
    gate_ref[...] = gate
    rank_ref[...] = rank.astype(jnp.int32)


def _route(logits_t, router_bias):
    m = logits_t.shape[1]
    tile = ROUTE_TILE
    j = lax.broadcasted_iota(jnp.int32, (tile, tile), 0)
    t = lax.broadcasted_iota(jnp.int32, (tile, tile), 1)
    uo = jnp.concatenate([(j < t).astype(BF16), jnp.ones((tile, tile), BF16)], axis=1)
    bias = jnp.broadcast_to(router_bias[:, None], (N_EXPERTS, tile))
    col = lambda i: (0, i)
    const = lambda i: (0, 0)
    idx, gate, rank, cnt = pl.pallas_call(
        _route_kernel,
        out_shape=(jax.ShapeDtypeStruct((TOP_K, m), jnp.int32),
                   jax.ShapeDtypeStruct((TOP_K, m), F32),
                   jax.ShapeDtypeStruct((TOP_K, m), jnp.int32),
                   jax.ShapeDtypeStruct((N_EXPERTS, tile), F32)),
        grid=(m // tile,),
        in_specs=[pl.BlockSpec((N_EXPERTS, tile), col),
                  pl.BlockSpec((N_EXPERTS, tile), const),
                  pl.BlockSpec((tile, 2 * tile), const)],
        out_specs=(pl.BlockSpec((TOP_K, tile), col),
                   pl.BlockSpec((TOP_K, tile), col),
                   pl.BlockSpec((TOP_K, tile), col),
                   pl.BlockSpec((N_EXPERTS, tile), const)),
        compiler_params=_params("arbitrary"),
        name="moe_route",
    )(logits_t, bias, uo)
    return idx, gate, rank, cnt[:, 0].astype(jnp.int32)


def _dispatch(idx, rank, counts):
    m = idx.shape[1]
    r = MOE_ROWS
    padded = (counts + r - 1) // r * r
    pend = jnp.cumsum(padded)
    pstart = pend - padded
    experts = jnp.arange(N_EXPERTS, dtype=jnp.int32)
    dest = jnp.sum(jnp.where(idx[:, :, None] == experts, pstart.astype(jnp.int32), 0), axis=-1) + rank
    nb = -(-(m * TOP_K) // r) + N_EXPERTS
    blk_lo = jnp.arange(nb, dtype=jnp.int32) * r
    blk_e = jnp.minimum(jnp.sum(pend[None, :] <= blk_lo[:, None], axis=1), N_EXPERTS - 1).astype(jnp.int32)
    nvalid = jnp.clip((pstart + counts)[blk_e] - blk_lo, 0, r).astype(jnp.int32)
    pad_blk = jnp.concatenate([jnp.maximum(pend // r - 1, 0), pend[-1:] // r]).astype(jnp.int32)
    return dest, blk_e, nvalid, pad_blk


def _scatter_kernel(last_blk_ref, dest_ref, xa_ref, xb_ref, xg_ref, zero_ref, sem, zsem, *, tile, steps_a):
    i = pl.program_id(0)

    @pl.when(i == 0)
    def _():
        zero_ref[...] = jnp.zeros_like(zero_ref)

        def zero_copy(blk):
            start = pl.multiple_of(blk * MOE_ROWS, MOE_ROWS)
            return pltpu.make_async_copy(zero_ref, xg_ref.at[pl.ds(start, MOE_ROWS), :], zsem)

        def start(e, carry):
            zero_copy(last_blk_ref[e]).start()
            return carry

        def wait(e, carry):
            zero_copy(last_blk_ref[e]).wait()
            return carry

        lax.fori_loop(0, N_EXPERTS, start, 0)
        lax.fori_loop(0, N_EXPERTS, wait, 0)

        def start_tail(blk, carry):
            zero_copy(blk).start()
            return carry

        def wait_tail(blk, carry):
            zero_copy(blk).wait()
            return carry

        used = last_blk_ref[N_EXPERTS]
        lax.fori_loop(used, xg_ref.shape[0] // MOE_ROWS, start_tail, 0)
        lax.fori_loop(used, xg_ref.shape[0] // MOE_ROWS, wait_tail, 0)

    def push(x_ref):
        def body(t, carry):
            for k in range(TOP_K):
                pltpu.make_async_copy(x_ref.at[pl.ds(t, 1), :], xg_ref.at[pl.ds(dest_ref[k, t], 1), :],
                                      sem).start(priority=k % 2)
            return carry
        lax.fori_loop(0, tile, body, 0)
        for k in range(TOP_K):
            pltpu.make_async_copy(x_ref, xg_ref.at[pl.ds(0, tile), :], sem).wait()

    @pl.when(i < steps_a)
    def _():
        push(xa_ref)

    @pl.when(i >= steps_a)
    def _():
        push(xb_ref)


def _moe_scatter(last_blk, dest, x_a, x_b, cap):
    tile = ROW_TILE
    width = x_a.shape[1]
    steps_a, steps_b = x_a.shape[0] // tile, x_b.shape[0] // tile
    return pl.pallas_call(
        functools.partial(_scatter_kernel, tile=tile, steps_a=steps_a),
        out_shape=jax.ShapeDtypeStruct((cap, width), x_a.dtype),
        grid=(steps_a + steps_b,),
        in_specs=[pl.BlockSpec(memory_space=pltpu.SMEM),
                  pl.BlockSpec((None, TOP_K, tile), lambda i: (i, 0, 0), memory_space=pltpu.SMEM),
                  pl.BlockSpec((tile, width), lambda i: (jnp.minimum(i, steps_a - 1), 0)),
                  pl.BlockSpec((tile, width), lambda i: (jnp.maximum(i - steps_a, 0), 0))],
        out_specs=pl.BlockSpec(memory_space=pl.ANY),
        scratch_shapes=[pltpu.VMEM((MOE_ROWS, width), x_a.dtype), pltpu.SemaphoreType.DMA(()),
                        pltpu.SemaphoreType.DMA(())],
        compiler_params=_params("arbitrary"),
        name="moe_scatter",
    )(last_blk, dest, x_a, x_b)


def _moe_kernel(blk_e_ref, nvalid_ref, x_ref, wg_ref, wu_ref, wd_ref, o_ref, wgb, wub, wdb):
    i = pl.program_id(0)
    e = blk_e_ref[i]
    prev = blk_e_ref[jnp.maximum(i - 1, 0)]

    @pl.when((i == 0) | (e != prev))
    def _():
        wgb[...] = wg_ref[...].astype(BF16)
        wub[...] = wu_ref[...].astype(BF16)
        wdb[...] = wd_ref[...].astype(BF16)

    @pl.when(nvalid_ref[i] > 0)
    def _():
        x = x_ref[...].astype(BF16)
        act = (_silu(_dot(x, wgb[...])) * _dot(x, wub[...])).astype(BF16)
        o_ref[...] = _dot(act, wdb[...])

    @pl.when(nvalid_ref[i] == 0)
    def _():
        o_ref[...] = jnp.zeros_like(o_ref)


def _moe_experts(blk_e, nvalid, xg, w_gate_e, w_up_e, w_down_e):
    cap = xg.shape[0]
    r = MOE_ROWS
    grid_spec = pltpu.PrefetchScalarGridSpec(
        num_scalar_prefetch=2,
        grid=(cap // r,),
        in_specs=[pl.BlockSpec((r, D_MODEL), lambda i, be, nv: (i, 0)),
                  pl.BlockSpec((None, D_MODEL, D_EXPERT), lambda i, be, nv: (be[i], 0, 0)),
                  pl.BlockSpec((None, D_MODEL, D_EXPERT), lambda i, be, nv: (be[i], 0, 0)),
                  pl.BlockSpec((None, D_EXPERT, D_MODEL), lambda i, be, nv: (be[i], 0, 0))],
        out_specs=pl.BlockSpec((r, D_MODEL), lambda i, be, nv: (i, 0)),
        scratch_shapes=[pltpu.VMEM((D_MODEL, D_EXPERT), BF16),
                        pltpu.VMEM((D_MODEL, D_EXPERT), BF16),
                        pltpu.VMEM((D_EXPERT, D_MODEL), BF16)])
    return pl.pallas_call(
        _moe_kernel,
        out_shape=jax.ShapeDtypeStruct((cap, D_MODEL), F32),
        grid_spec=grid_spec,
        compiler_params=_params("arbitrary"),
        name="moe_experts",
    )(blk_e, nvalid, xg, w_gate_e, w_up_e, w_down_e)


def _final_kernel(dcur_ref, dnext_ref, x1_ref, gate_ref, rows_ref, wg_ref, wu_ref, wd_ref, g_ref, b_ref,
                  o_ref, buf, sem, *, tile):
    i = pl.program_id(0)
    n = pl.num_programs(0)
    slot = i % 2

    def gather(dest_ref, s):
        def body(t, carry):
            for k in range(TOP_K):
                pltpu.make_async_copy(rows_ref.at[pl.ds(dest_ref[k, t], 1), :],
                                      buf.at[s, k, pl.ds(t, 1), :], sem.at[s]).start(priority=k % 2)
            return carry
        lax.fori_loop(0, tile, body, 0)

    @pl.when(i == 0)
    def _():
        gather(dcur_ref, 0)

    @pl.when(i + 1 < n)
    def _():
        gather(dnext_ref, 1 - slot)

    x1 = x1_ref[...]
    xb = x1.astype(BF16)
    act = (_silu(_dot(xb, wg_ref[...])) * _dot(xb, wu_ref[...])).astype(BF16)
    moe = _dot(act, wd_ref[...])
    for k in range(TOP_K):
        pltpu.make_async_copy(rows_ref.at[pl.ds(0, tile), :], buf.at[slot, k], sem.at[slot]).wait()
    gate = gate_ref[...]
    for k in range(TOP_K):
        moe = moe + buf[slot, k] * gate[:, k:k + 1]
    o_ref[...] = _layer_norm(ALPHA * x1 + moe, g_ref[...], b_ref[...])


def _final(x1, dest, gates_t, rows, row_off, w_gate_s, w_up_s, w_down_s, ln_g, ln_b):
    m = x1.shape[0]
    tile = ROW_TILE
    n = m // tile
    off = row_off // tile
    row = lambda i: (i, 0)
    const = lambda i: (0, 0)
    return pl.pallas_call(
        functools.partial(_final_kernel, tile=tile),
        out_shape=jax.ShapeDtypeStruct((m, D_MODEL), F32),
        grid=(n,),
        in_specs=[pl.BlockSpec((None, TOP_K, tile), lambda i: (off + i, 0, 0), memory_space=pltpu.SMEM),
                  pl.BlockSpec((None, TOP_K, tile), lambda i: (off + jnp.minimum(i + 1, n - 1), 0, 0),
                               memory_space=pltpu.SMEM),
                  pl.BlockSpec((tile, D_MODEL), row),
                  pl.BlockSpec((tile, TOP_K), lambda i: (off + i, 0)),
                  pl.BlockSpec(memory_space=pl.ANY),
                  pl.BlockSpec((D_MODEL, D_EXPERT), const),
                  pl.BlockSpec((D_MODEL, D_EXPERT), const),
                  pl.BlockSpec((D_EXPERT, D_MODEL), const),
                  pl.BlockSpec((1, D_MODEL), const),
                  pl.BlockSpec((1, D_MODEL), const)],
        out_specs=pl.BlockSpec((tile, D_MODEL), row),
        scratch_shapes=[pltpu.VMEM((2, TOP_K, tile, D_MODEL), F32), pltpu.SemaphoreType.DMA((2,))],
        compiler_params=_params("arbitrary"),
        name="shared_ln",
    )(dest, dest, x1, gates_t, rows, w_gate_s, w_up_s, w_down_s, ln_g[None, :], ln_b[None, :])


def kernel(x_prompt, x_sample, cache_k, cache_v, state_conv, state_ssm, page_table, w_in, sb_bias, conv_w, conv_b, dt_bias, a_log, d_skip, g_attn, g_ssd, w_out, ln1_g, ln1_b, w_router, router_bias, w_gate_e, w_up_e, w_down_e, w_gate_s, w_up_s, w_down_s, ln2_g, ln2_b):
    assert w_in.shape[0] == 1, "one layer"
    bp, seq, _ = x_prompt.shape
    bs, dec, _ = x_sample.shape
    assert dec >= CONV_WIDTH - 1
    mp, ms = bp * seq, bs * dec
    n_phys = cache_k.shape[1]

    w_in_b = jnp.pad(w_in[0], ((0, 0), (0, IN_PAD - w_in.shape[2]))).astype(BF16)
    w_out_b = w_out[0].astype(BF16)
    w_router_t = w_router[0].T.astype(BF16)
    mix_w = (g_attn[0], g_ssd[0], w_out_b, ln1_g[0], ln1_b[0], w_router_t)
    ssd_w = (conv_w[0], conv_b[0], dt_bias[0], a_log[0], d_skip[0])
    dt_blk = IN_MAIN // LANES
    xbc_blk = (3 * ATTN_WIDTH + SSM_WIDTH) // CONV_DIM

    xp = x_prompt.reshape(mp, D_MODEL)
    proj_p = _proj(xp, w_in_b, 1024, 896)
    proj_p3 = proj_p.reshape(bp, seq, IN_PAD)
    k_prompt = proj_p3[:, :, ATTN_WIDTH:2 * ATTN_WIDTH].reshape(1, bp, seq, N_HEADS, HEAD_DIM)
    v_prompt = proj_p3[:, :, 2 * ATTN_WIDTH:3 * ATTN_WIDTH].reshape(1, bp, seq, N_HEADS, HEAD_DIM)
    conv_prompt = proj_p3[:, seq - (CONV_WIDTH - 1):, IN_MAIN - CONV_DIM:IN_MAIN][None]
    oa_p = _attn_prompt(proj_p, sb_bias[0], bp, seq)
    dtr_p = proj_p3[:, :, IN_MAIN:IN_MAIN + N_SSM_HEADS].transpose(0, 2, 1)
    y_p, h_p = _ssd(proj_p3, xbc_blk, jnp.zeros((bp, SUBLANES, CONV_DIM), F32),
                    jnp.zeros((bp, N_SSM_HEADS, SSM_HEAD_DIM, SSM_STATE), F32),
                    proj_p3, dt_blk, dtr_p, *ssd_w, rows_in=SSD_CHUNK, valid=SSD_CHUNK)
    x1_p, lg_p = _mix_out(oa_p, y_p.reshape(mp, SSM_WIDTH), proj_p, xp, *mix_w, tm=256)

    xs = x_sample.reshape(ms, D_MODEL)
    proj_s = _proj(xs, w_in_b, ms, 896)
    proj_s3 = proj_s.reshape(bs, dec, IN_PAD)
    k_sample = proj_s3[:, :, ATTN_WIDTH:2 * ATTN_WIDTH].reshape(1, bs, dec, N_HEADS, HEAD_DIM)
    v_sample = proj_s3[:, :, 2 * ATTN_WIDTH:3 * ATTN_WIDTH].reshape(1, bs, dec, N_HEADS, HEAD_DIM)
    conv_sample = proj_s3[:, dec - (CONV_WIDTH - 1):, IN_MAIN - CONV_DIM:IN_MAIN][None]
    proj_s8 = jnp.pad(proj_s3, ((0, 0), (0, SUBLANES - dec), (0, 0)))
    oa_s = _attn_sample(proj_s8, cache_k.reshape(n_phys, PAGE_SIZE * N_HEADS, HEAD_DIM),
                        cache_v.reshape(n_phys, PAGE_SIZE * N_HEADS, HEAD_DIM), page_table, sb_bias[0])
    dtc_s = jnp.pad(proj_s8[:, :, IN_MAIN:], ((0, 0), (0, SSD_CHUNK - SUBLANES), (0, 0)))
    dtr_s = dtc_s[:, :, :N_SSM_HEADS].transpose(0, 2, 1)
    conv0_s = jnp.pad(state_conv[0], ((0, 0), (SUBLANES - (CONV_WIDTH - 1), 0), (0, 0)))
    y_s, h_s = _ssd(proj_s8, xbc_blk, conv0_s, state_ssm[0], dtc_s, 0, dtr_s, *ssd_w,
                    rows_in=SUBLANES, valid=dec)
    x1_s, lg_s = _mix_out(oa_s[:, :dec].reshape(ms, ATTN_WIDTH), y_s[:, :dec].reshape(ms, SSM_WIDTH),
                                 proj_s, xs, *mix_w, tm=ms)

    m = mp + ms
    idx, gates, rank, counts = _route(jnp.concatenate([lg_p, lg_s], axis=1), router_bias[0])
    dest, blk_e, nvalid, pad_blk = _dispatch(idx, rank, counts)
    dest_t = dest.reshape(TOP_K, m // ROW_TILE, ROW_TILE).transpose(1, 0, 2)
    xg = _moe_scatter(pad_blk, dest_t, x1_p, x1_s, blk_e.shape[0] * MOE_ROWS)
    out_rows = _moe_experts(blk_e, nvalid, xg, w_gate_e[0], w_up_e[0], w_down_e[0])
    shared_w = (w_gate_s[0].astype(BF16), w_up_s[0].astype(BF16), w_down_s[0].astype(BF16), ln2_g[0], ln2_b[0])
    y_p = _final(x1_p, dest_t, gates.T, out_rows, 0, *shared_w)
    y_s = _final(x1_s, dest_t, gates.T, out_rows, mp, *shared_w)

    return (y_p.reshape(bp, seq, D_MODEL), y_s.reshape(bs, dec, D_MODEL),
            k_prompt, v_prompt, conv_prompt, h_p[None],
            k_sample, v_sample, conv_sample, h_s[None])
```

```python
import functools

import jax
import jax.numpy as jnp
from jax import lax
from jax.experimental import pallas as pl
from jax.experimental.pallas import tpu as pltpu

F32 = jnp.float32
BF16 = jnp.bfloat16

D_MODEL = 2048
ATTN_WIDTH = 1024
SSM_WIDTH = 1024
HEAD_DIM = 128
N_HEADS = 8
SB_SCALE = HEAD_DIM ** -0.5
SSM_HEAD_DIM = 64
N_SSM_HEADS = 16
SSM_STATE = 128
N_GROUPS = 4
HEADS_PER_GROUP = N_SSM_HEADS // N_GROUPS
CONV_WIDTH = 4
CONV_DIM = SSM_WIDTH + 2 * N_GROUPS * SSM_STATE
IN_MAIN = 3 * ATTN_WIDTH + SSM_WIDTH + CONV_DIM
IN_PAD = IN_MAIN + 128
N_EXPERTS = 64
N_EXPERT_GROUPS = 8
EXPERTS_PER_GROUP = N_EXPERTS // N_EXPERT_GROUPS
TOPK_GROUPS = 4
TOP_K = 8
D_EXPERT = 512
ROUTED_SCALE = 2.5
ALPHA = 2.0 ** 0.25
LN_EPS = 1e-5
PAGE_SIZE = 128

LANES = 128
SUBLANES = 8
VMEM_LIMIT = 56 * 1024 * 1024

SSD_CHUNK = 128
ATTN_BLOCK = 256
ATTN_HEADS_PER_STEP = 4
PAGES_PER_STEP = 8
PAGE_RING = 3
MOE_ROWS = 512
ROUTE_TILE = 128
ROW_TILE = 128
NEG_BIG = -1e30
LOG2E = 1.4426950408889634


def _params(*sem):
    return pltpu.CompilerParams(dimension_semantics=sem, vmem_limit_bytes=VMEM_LIMIT)


def _softplus(x):
    return jnp.maximum(x, 0.0) + jnp.log1p(jnp.exp(-jnp.abs(x)))


def _silu(x):
    return x * (1.0 / (1.0 + jnp.exp(-x)))


def _dot(a, b, **kw):
    return jnp.dot(a, b, preferred_element_type=F32, **kw)


def _dot_nt(a, b):
    return lax.dot_general(a, b, (((1,), (1,)), ((), ())), preferred_element_type=F32)


def _dot_tn(a, b):
    return lax.dot_general(a, b, (((0,), (0,)), ((), ())), preferred_element_type=F32)


def _proj_kernel(x_ref, w_ref, o_ref):
    o_ref[...] = _dot(x_ref[...].astype(BF16), w_ref[...])


def _proj(x, w, tm, tn):
    m, d = x.shape
    n = w.shape[1]
    return pl.pallas_call(
        _proj_kernel,
        out_shape=jax.ShapeDtypeStruct((m, n), F32),
        grid=(m // tm, n // tn),
        in_specs=[pl.BlockSpec((tm, d), lambda i, j: (i, 0)),
                  pl.BlockSpec((d, tn), lambda i, j: (0, j))],
        out_specs=pl.BlockSpec((tm, tn), lambda i, j: (i, j)),
        compiler_params=_params("parallel", "arbitrary"),
        name="in_proj",
    )(x, w)


def _neg_suffix(n):
    j = lax.broadcasted_iota(jnp.int32, (n, n), 0)
    s = lax.broadcasted_iota(jnp.int32, (n, n), 1)
    return -jnp.concatenate([(j > s).astype(BF16), jnp.ones((n, LANES), BF16)], axis=1)


def _sb_prep(z, valid, uo):
    sp = jnp.maximum(z, 0.0) + jnp.log(1.0 + jnp.exp2(jnp.abs(z) * -LOG2E))
    spm = sp if valid is None else jnp.where(valid, sp, 0.0)
    return z - sp, _dot(spm.astype(BF16), uo)


def _sb_weights(log_beta, lc, valid, c, nk):
    cb = c if nk == LANES else jnp.concatenate([c] * (nk // LANES), axis=1)
    a = jnp.exp2((log_beta + lc[:, :nk] + cb) * LOG2E)
    if valid is not None:
        a = jnp.where(valid, a, 0.0)
    return a, c + lc[:, nk:]


def _sb_block(z, valid, c, uo, nk):
    log_beta, lc = _sb_prep(z, valid, uo)
    return _sb_weights(log_beta, lc, valid, c, nk)


def _attn_prompt_kernel(bias_ref, q_ref, k_ref, v_ref, uo_ref, o_ref, kb_ref, vb_ref, *, blk, heads):
    hg = pl.program_id(1)
    i = pl.program_id(2)

    @pl.when(i == 0)
    def _():
        kb_ref[...] = k_ref[...].astype(BF16)
        vb_ref[...] = v_ref[...].astype(BF16)

    uo = uo_ref[...]
    row = lax.broadcasted_iota(jnp.int32, (blk, blk), 0)
    col = lax.broadcasted_iota(jnp.int32, (blk, blk), 1)
    qs = [q_ref[:, h * HEAD_DIM:(h + 1) * HEAD_DIM].astype(BF16) for h in range(heads)]
    biases = [bias_ref[hg * heads + h] for h in range(heads)]

    def block(j, carry, valid):
        start = pl.multiple_of(j * blk, blk)
        out = []
        for h in range(heads):
            c, acc = carry[h]
            lanes = slice(h * HEAD_DIM, (h + 1) * HEAD_DIM)
            z = _dot_nt(qs[h], kb_ref[pl.ds(start, blk), lanes]) * SB_SCALE + biases[h]
            a, c = _sb_block(z, valid, c, uo, blk)
            out.append((c, acc + _dot(a.astype(BF16), vb_ref[pl.ds(start, blk), lanes])))
        return tuple(out)

    zero = jnp.zeros((blk, HEAD_DIM), F32)
    carry = block(i, ((zero, zero),) * heads, col < row)
    carry = lax.fori_loop(0, i, lambda jj, cr: block(i - 1 - jj, cr, None), carry)
    for h in range(heads):
        o_ref[:, h * HEAD_DIM:(h + 1) * HEAD_DIM] = carry[h][1]


def _attn_prompt(proj, sb_bias, batch, seq):
    blk = ATTN_BLOCK
    heads = ATTN_HEADS_PER_STEP
    nq = seq // blk
    ng = N_HEADS // heads
    width = heads * HEAD_DIM
    return pl.pallas_call(
        functools.partial(_attn_prompt_kernel, blk=blk, heads=heads),
        out_shape=jax.ShapeDtypeStruct((batch * seq, ATTN_WIDTH), F32),
        grid=(batch, ng, nq),
        in_specs=[pl.BlockSpec(memory_space=pltpu.SMEM),
                  pl.BlockSpec((blk, width), lambda b, g, i: (b * nq + i, g)),
                  pl.BlockSpec((seq, width), lambda b, g, i: (b, ng + g)),
                  pl.BlockSpec((seq, width), lambda b, g, i: (b, 2 * ng + g)),
                  pl.BlockSpec((blk, blk + LANES), lambda b, g, i: (0, 0))],
        out_specs=pl.BlockSpec((blk, width), lambda b, g, i: (b * nq + i, g)),
        scratch_shapes=[pltpu.VMEM((seq, width), BF16), pltpu.VMEM((seq, width), BF16)],
        compiler_params=_params("parallel", "parallel", "arbitrary"),
        name="sb_attn_prompt",
    )(sb_bias, proj, proj, proj, _neg_suffix(blk))


def _attn_sample_kernel(pt_ref, bias_ref, q_ref, kn_ref, vn_ref, ck_ref, cv_ref, uo_ref, o_ref,
                        kbuf, vbuf, sem, *, pps, nbuf, n_pages):
    b = pl.program_id(0)
    steps = n_pages // pps
    rows = N_HEADS * SUBLANES
    uo = uo_ref[...]
    qb = q_ref[...].astype(BF16)
    rh = lax.broadcasted_iota(jnp.int32, (rows, PAGE_SIZE), 0) // SUBLANES
    bias_v = jnp.zeros((rows, PAGE_SIZE), F32)
    for h in range(N_HEADS):
        bias_v = jnp.where(rh == h, bias_ref[h], bias_v)

    def page_copies(seq, step, slot):
        out = []
        for u in range(pps):
            page = pt_ref[seq, n_pages - 1 - (step * pps + u)]
            out.append(pltpu.make_async_copy(ck_ref.at[page], kbuf.at[slot, u], sem.at[slot]))
            out.append(pltpu.make_async_copy(cv_ref.at[page], vbuf.at[slot, u], sem.at[slot]))
        return out

    def sweep(blocks, valid, c, acc):
        zs = [jnp.concatenate([_dot_nt(qb[:, h * HEAD_DIM:(h + 1) * HEAD_DIM], load_k(h))
                               for h in range(N_HEADS)], axis=0) * SB_SCALE + bias_v
              for load_k, _ in blocks]
        preps = [_sb_prep(z, valid, uo) for z in zs]
        weights = []
        for log_beta, lc in preps:
            a, c = _sb_weights(log_beta, lc, valid, c, PAGE_SIZE)
            weights.append(a)
        for a, (_, load_v) in zip(weights, blocks):
            acc = acc + jnp.concatenate(
                [_dot(a[h * SUBLANES:(h + 1) * SUBLANES].astype(BF16), load_v(h)) for h in range(N_HEADS)], axis=0)
        return c, acc

    @pl.when(b == 0)
    def _():
        for step in range(nbuf - 1):
            for cp in page_copies(0, step, step):
                cp.start()

    pad = jnp.zeros((PAGE_SIZE - SUBLANES, HEAD_DIM), F32)

    def new_rows(ref):
        return lambda h: jnp.concatenate([ref[:, h * HEAD_DIM:(h + 1) * HEAD_DIM], pad], axis=0).astype(BF16)

    t = lax.broadcasted_iota(jnp.int32, (rows, PAGE_SIZE), 0) % SUBLANES
    key = lax.broadcasted_iota(jnp.int32, (rows, PAGE_SIZE), 1)
    carry = sweep([(new_rows(kn_ref), new_rows(vn_ref))], key < t,
                  jnp.zeros((rows, LANES), F32), jnp.zeros((rows, HEAD_DIM), F32))

    def page_rows(buf, slot, u):
        return lambda h: buf[slot, u, pl.ds(h, PAGE_SIZE, stride=N_HEADS), :].astype(BF16)

    def body(step, carry):
        g = b * steps + step
        slot = g % nbuf
        ahead = step + (nbuf - 1)
        ahead_seq = b + ahead // steps

        @pl.when(ahead_seq < pl.num_programs(0))
        def _():
            for cp in page_copies(ahead_seq, ahead % steps, (g + (nbuf - 1)) % nbuf):
                cp.start()

        for cp in page_copies(b, step, slot):
            cp.wait()
        return sweep([(page_rows(kbuf, slot, u), page_rows(vbuf, slot, u)) for u in range(pps)], None, *carry)

    _, acc = lax.fori_loop(0, steps, body, carry)
    for h in range(N_HEADS):
        o_ref[:, h * HEAD_DIM:(h + 1) * HEAD_DIM] = acc[h * SUBLANES:(h + 1) * SUBLANES, :]


def _attn_sample(proj8, cache_k, cache_v, page_table, sb_bias):
    batch = proj8.shape[0]
    n_pages = page_table.shape[1]
    pps, nbuf = PAGES_PER_STEP, PAGE_RING
    assert n_pages % pps == 0 and n_pages // pps >= nbuf

    def tok_spec(col):
        return pl.BlockSpec((None, SUBLANES, ATTN_WIDTH), lambda b, pt: (b, 0, col))

    grid_spec = pltpu.PrefetchScalarGridSpec(
        num_scalar_prefetch=1,
        grid=(batch,),
        in_specs=[pl.BlockSpec(memory_space=pltpu.SMEM), tok_spec(0), tok_spec(1), tok_spec(2),
                  pl.BlockSpec(memory_space=pl.ANY), pl.BlockSpec(memory_space=pl.ANY),
                  pl.BlockSpec((PAGE_SIZE, 2 * LANES), lambda b, pt: (0, 0))],
        out_specs=pl.BlockSpec((None, SUBLANES, ATTN_WIDTH), lambda b, pt: (b, 0, 0)),
        scratch_shapes=[pltpu.VMEM((nbuf, pps, PAGE_SIZE * N_HEADS, HEAD_DIM), F32),
                        pltpu.VMEM((nbuf, pps, PAGE_SIZE * N_HEADS, HEAD_DIM), F32),
                        pltpu.SemaphoreType.DMA((nbuf,))])
    return pl.pallas_call(
        functools.partial(_attn_sample_kernel, pps=pps, nbuf=nbuf, n_pages=n_pages),
        out_shape=jax.ShapeDtypeStruct((batch, SUBLANES, ATTN_WIDTH), F32),
        grid_spec=grid_spec,
        compiler_params=_params("arbitrary"),
        name="sb_attn_sample",
    )(page_table, sb_bias, proj8, proj8, proj8, cache_k, cache_v, _neg_suffix(PAGE_SIZE))


def _ssd_kernel(x_ref, conv0_ref, h0_ref, dtc_ref, dtr_ref, cw_ref, cb_ref,
                dtb_c_ref, alog_c_ref, dtb_r_ref, alog_r_ref, dskip_ref, y_ref, hout_ref, ext_ref,
                *, rows_in, valid):
    q = SSD_CHUNK
    c = pl.program_id(1)

    @pl.when(c == 0)
    def _():
        ext_ref[0:SUBLANES, :] = conv0_ref[...]
        hout_ref[...] = h0_ref[...]

    if rows_in < q:
        ext_ref[SUBLANES:, :] = jnp.zeros((q, CONV_DIM), F32)
    ext_ref[SUBLANES:SUBLANES + rows_in, :] = x_ref[...]
    cw = cw_ref[...]
    xc = cb_ref[...] + ext_ref[5:5 + q, :] * cw[0:1, :]
    xc = xc + ext_ref[6:6 + q, :] * cw[1:2, :]
    xc = xc + ext_ref[7:7 + q, :] * cw[2:3, :]
    xc = _silu(xc + ext_ref[8:8 + q, :] * cw[3:4, :])
    ext_ref[0:SUBLANES, :] = ext_ref[q:q + SUBLANES, :]

    t_c = lax.broadcasted_iota(jnp.int32, (q, LANES), 0)
    h_c = lax.broadcasted_iota(jnp.int32, (q, LANES), 1)
    dt_c = jnp.where((t_c < valid) & (h_c < N_SSM_HEADS), _softplus(dtc_ref[...] + dtb_c_ref[...]), 0.0)
    t_r = lax.broadcasted_iota(jnp.int32, (N_SSM_HEADS, q), 1)
    dt_r = jnp.where(t_r < valid, _softplus(dtr_ref[...] + dtb_r_ref[...]), 0.0)
    ti = lax.broadcasted_iota(jnp.int32, (q, q), 0)
    tj = lax.broadcasted_iota(jnp.int32, (q, q), 1)
    causal = tj <= ti
    acum_c = _dot(causal.astype(F32), dt_c * -jnp.exp(alog_c_ref[...]), precision=lax.Precision.HIGHEST)
    acum_r = _dot(dt_r * -jnp.exp(alog_r_ref[...]), (ti <= tj).astype(F32), precision=lax.Precision.HIGHEST)

    first = lax.broadcasted_iota(jnp.int32, (q, LANES), 1) < SSM_HEAD_DIM
    first_row = lax.broadcasted_iota(jnp.int32, (2 * SSM_HEAD_DIM, SSM_STATE), 0) < SSM_HEAD_DIM
    for g in range(N_GROUPS):
        lo = SSM_WIDTH + g * SSM_STATE
        bg = xc[:, lo:lo + SSM_STATE].astype(BF16)
        lo = SSM_WIDTH + (N_GROUPS + g) * SSM_STATE
        cg = xc[:, lo:lo + SSM_STATE].astype(BF16)
        cb = _dot_nt(cg, bg)
        for pair in range(g * HEADS_PER_GROUP // 2, (g + 1) * HEADS_PER_GROUP // 2):
            heads = (2 * pair, 2 * pair + 1)
            ac = [jnp.broadcast_to(acum_c[:, h:h + 1], (q, LANES)) for h in heads]
            dt = [jnp.broadcast_to(dt_c[:, h:h + 1], (q, LANES)) for h in heads]
            m = [(cb * jnp.exp(jnp.where(causal, ac[k] - acum_r[h:h + 1, :], NEG_BIG))).astype(BF16)
                 for k, h in enumerate(heads)]
            acp = jnp.where(first, ac[0], ac[1])
            dtp = jnp.where(first, dt[0], dt[1])
            lanes = slice(pair * LANES, (pair + 1) * LANES)
            xs = xc[:, lanes]
            xdt = xs * dtp
            y = _dot(m[0], jnp.where(first, xdt, 0.0).astype(BF16))
            y = y + _dot(m[1], jnp.where(first, 0.0, xdt).astype(BF16))
            hst = hout_ref[pair]
            y = y + _dot_nt(cg, hst.astype(BF16)) * jnp.exp(acp)
            y = y + dskip_ref[:, lanes] * xs
            xw = (xs * (jnp.exp(acp[q - 1:q, :] - acp) * dtp)).astype(BF16)
            keep = jnp.where(first_row, jnp.exp(acum_r[heads[0]:heads[0] + 1, q - 1:q]),
                             jnp.exp(acum_r[heads[1]:heads[1] + 1, q - 1:q]))
            hout_ref[pair] = hst * keep + _dot_tn(xw, bg)
            y_ref[:, lanes] = y[:rows_in]


def _ssd(x3, col_blk, conv0, h0, dtc, dtc_blk, dtr, conv_w, conv_b, dt_bias, a_log, d_skip, rows_in, valid):
    batch, length = x3.shape[0], x3.shape[1]
    nc = max(length // SSD_CHUNK, 1)
    lane_pad = (0, LANES - N_SSM_HEADS)
    dtb_c = jnp.pad(dt_bias, lane_pad)[None, :]
    alog_c = jnp.pad(a_log, lane_pad)[None, :]
    dtb_r = jnp.broadcast_to(dt_bias[:, None], (N_SSM_HEADS, LANES))
    alog_r = jnp.broadcast_to(a_log[:, None], (N_SSM_HEADS, LANES))
    dskip = jnp.repeat(d_skip, SSM_HEAD_DIM)[None, :]
    const2 = lambda b, c: (0, 0)
    pairs = N_SSM_HEADS // 2
    state_shape = (batch, pairs, 2 * SSM_HEAD_DIM, SSM_STATE)
    state_spec = pl.BlockSpec((None, pairs, 2 * SSM_HEAD_DIM, SSM_STATE), lambda b, c: (b, 0, 0, 0))
    y, h = pl.pallas_call(
        functools.partial(_ssd_kernel, rows_in=rows_in, valid=valid),
        out_shape=(jax.ShapeDtypeStruct((batch, length, SSM_WIDTH), F32),
                   jax.ShapeDtypeStruct(state_shape, F32)),
        grid=(batch, nc),
        in_specs=[pl.BlockSpec((None, rows_in, CONV_DIM), lambda b, c: (b, c, col_blk)),
                  pl.BlockSpec((None, SUBLANES, CONV_DIM), lambda b, c: (b, 0, 0)),
                  state_spec,
                  pl.BlockSpec((None, SSD_CHUNK, LANES), lambda b, c: (b, c, dtc_blk)),
                  pl.BlockSpec((None, N_SSM_HEADS, SSD_CHUNK), lambda b, c: (b, 0, c)),
                  pl.BlockSpec((CONV_WIDTH, CONV_DIM), const2),
                  pl.BlockSpec((1, CONV_DIM), const2),
                  pl.BlockSpec((1, LANES), const2),
                  pl.BlockSpec((1, LANES), const2),
                  pl.BlockSpec((N_SSM_HEADS, LANES), const2),
                  pl.BlockSpec((N_SSM_HEADS, LANES), const2),
                  pl.BlockSpec((1, SSM_WIDTH), const2)],
        out_specs=(pl.BlockSpec((None, rows_in, SSM_WIDTH), lambda b, c: (b, c, 0)), state_spec),
        scratch_shapes=[pltpu.VMEM((SSD_CHUNK + SUBLANES, CONV_DIM), F32)],
        compiler_params=_params("parallel", "arbitrary"),
        name="ssd_mixer",
    )(x3, conv0, h0.reshape(state_shape), dtc, dtr, conv_w, conv_b[None, :], dtb_c, alog_c, dtb_r, alog_r, dskip)
    return y, h.reshape(batch, N_SSM_HEADS, SSM_HEAD_DIM, SSM_STATE)


def _layer_norm(u, g, b):
    mu = jnp.mean(u, axis=-1, keepdims=True)
    d = u - mu
    var = jnp.mean(d * d, axis=-1, keepdims=True)
    return d * lax.rsqrt(var + LN_EPS) * g + b


def _mix_out_kernel(oa_ref, ys_ref, z_ref, x_ref, ga_ref, gs_ref, wo_ref, g1_ref, b1_ref, wr_ref,
                    x1_ref, lg_ref):
    oa = oa_ref[...]
    na = oa * lax.rsqrt(jnp.mean(oa * oa, axis=-1, keepdims=True) + LN_EPS) * ga_ref[...]
    m = _dot(na.astype(BF16), wo_ref[0:ATTN_WIDTH, :])
    yz = ys_ref[...] * _silu(z_ref[...])
    gw = SSM_WIDTH // N_GROUPS
    gs = gs_ref[...]
    for g in range(N_GROUPS):
        yg = yz[:, g * gw:(g + 1) * gw]
        ng = yg * lax.rsqrt(jnp.mean(yg * yg, axis=-1, keepdims=True) + LN_EPS) * gs[:, g * gw:(g + 1) * gw]
        lo = ATTN_WIDTH + g * gw
        m = m + _dot(ng.astype(BF16), wo_ref[lo:lo + gw, :])
    x1 = _layer_norm(ALPHA * x_ref[...] + m, g1_ref[...], b1_ref[...])
    x1_ref[...] = x1
    lg_ref[...] = _dot_nt(wr_ref[...], x1.astype(BF16))


def _mix_out(o_attn, y_ssd, proj, x, g_attn, g_ssd, w_out, ln_g, ln_b, w_router_t, tm):
    m = x.shape[0]
    row = lambda i: (i, 0)
    const = lambda i: (0, 0)
    return pl.pallas_call(
        _mix_out_kernel,
        out_shape=(jax.ShapeDtypeStruct((m, D_MODEL), F32),
                   jax.ShapeDtypeStruct((N_EXPERTS, m), F32)),
        grid=(m // tm,),
        in_specs=[pl.BlockSpec((tm, ATTN_WIDTH), row),
                  pl.BlockSpec((tm, SSM_WIDTH), row),
                  pl.BlockSpec((tm, SSM_WIDTH), lambda i: (i, 3)),
                  pl.BlockSpec((tm, D_MODEL), row),
                  pl.BlockSpec((1, ATTN_WIDTH), const),
                  pl.BlockSpec((1, SSM_WIDTH), const),
                  pl.BlockSpec((D_MODEL, D_MODEL), const),
                  pl.BlockSpec((1, D_MODEL), const),
                  pl.BlockSpec((1, D_MODEL), const),
                  pl.BlockSpec((N_EXPERTS, D_MODEL), const)],
        out_specs=(pl.BlockSpec((tm, D_MODEL), row),
                   pl.BlockSpec((N_EXPERTS, tm), lambda i: (0, i))),
        compiler_params=_params("parallel"),
        name="mix_out_ln",
    )(o_attn, y_ssd, proj, x, g_attn[None, :], g_ssd[None, :], w_out, ln_g[None, :], ln_b[None, :], w_router_t)


def _route_kernel(lg_ref, bias_ref, uo_ref, idx_ref, gate_ref, rank_ref, cnt_ref):
    per = EXPERTS_PER_GROUP
    tile = ROUTE_TILE
    ninf = -jnp.inf

    @pl.when(pl.program_id(0) == 0)
    def _():
        cnt_ref[...] = jnp.zeros_like(cnt_ref)

    scores = 1.0 / (1.0 + jnp.exp(-lg_ref[...]))
    choice = scores + bias_ref[...]
    sub = lax.broadcasted_iota(jnp.int32, (per, tile), 0)

    def first_max(x):
        m = jnp.max(x, axis=0, keepdims=True)
        return m, jnp.min(jnp.where(x == m, sub, per), axis=0, keepdims=True)

    sc = [scores[g * per:(g + 1) * per] for g in range(N_EXPERT_GROUPS)]
    ch = [choice[g * per:(g + 1) * per] for g in range(N_EXPERT_GROUPS)]

    gscore = jnp.zeros((N_EXPERT_GROUPS, tile), F32)
    for g in range(N_EXPERT_GROUPS):
        m1, i1 = first_max(ch[g])
        m2 = jnp.max(jnp.where(sub == i1, ninf, ch[g]), axis=0, keepdims=True)
        gscore = jnp.where(sub == g, m1 + m2, gscore)
    picked = jnp.zeros((N_EXPERT_GROUPS, tile), F32)
    for _ in range(TOPK_GROUPS):
        _, ig = first_max(gscore)
        hit = sub == ig
        picked = jnp.where(hit, 1.0, picked)
        gscore = jnp.where(hit, ninf, gscore)
    mc = [jnp.where(picked[g:g + 1, :] > 0.5, ch[g], ninf) for g in range(N_EXPERT_GROUPS)]

    eidx = [sub + per * g for g in range(N_EXPERT_GROUPS)]
    onehot = [jnp.zeros((per, tile), F32)] * N_EXPERT_GROUPS
    picks, weights = [], []
    for _ in range(TOP_K):
        mx = mc[0]
        for g in range(1, N_EXPERT_GROUPS):
            mx = jnp.maximum(mx, mc[g])
        m = jnp.max(mx, axis=0, keepdims=True)
        cand = jnp.where(mc[0] == m, eidx[0], N_EXPERTS)
        for g in range(1, N_EXPERT_GROUPS):
            cand = jnp.minimum(cand, jnp.where(mc[g] == m, eidx[g], N_EXPERTS))
        ix = jnp.min(cand, axis=0, keepdims=True)
        w = jnp.zeros((per, tile), F32)
        for g in range(N_EXPERT_GROUPS):
            hit = eidx[g] == ix
            w = w + jnp.where(hit, sc[g], 0.0)
            mc[g] = jnp.where(hit, ninf, mc[g])
            onehot[g] = jnp.where(hit, 1.0, onehot[g])
        picks.append(ix)
        weights.append(jnp.sum(w, axis=0, keepdims=True))
    wsum = weights[0]
    for k in range(1, TOP_K):
        wsum = wsum + weights[k]

    lc = _dot(jnp.concatenate(onehot, axis=0).astype(BF16), uo_ref[...])
    before = lc[:, :tile] + cnt_ref[...]
    cnt_ref[...] += lc[:, tile:]
    idx = jnp.zeros((TOP_K, tile), jnp.int32)
    gate = jnp.zeros((TOP_K, tile), F32)
    rank = jnp.zeros((TOP_K, tile), F32)
    for k in range(TOP_K):
        r = jnp.zeros((per, tile), F32)
        for g in range(N_EXPERT_GROUPS):
            r = r + jnp.where(eidx[g] == picks[k], before[g * per:(g + 1) * per], 0.0)
        idx = jnp.where(sub == k, picks[k], idx)
        gate = jnp.where(sub == k, weights[k] / wsum * ROUTED_SCALE, gate)
        rank = jnp.where(sub == k, jnp.sum(r, axis=0, keepdims=True), rank)
    idx_ref[...] = idx
    gate_ref[...] = gate
    rank_ref[...] = rank.astype(jnp.int32)


def _route(logits_t, router_bias):
    m = logits_t.shape[1]
    tile = ROUTE_TILE
    j = lax.broadcasted_iota(jnp.int32, (tile, tile), 0)
    t = lax.broadcasted_iota(jnp.int32, (tile, tile), 1)
    uo = jnp.concatenate([(j < t).astype(BF16), jnp.ones((tile, tile), BF16)], axis=1)
    bias = jnp.broadcast_to(router_bias[:, None], (N_EXPERTS, tile))
    col = lambda i: (0, i)
    const = lambda i: (0, 0)
    idx, gate, rank, cnt = pl.pallas_call(
        _route_kernel,
        out_shape=(jax.ShapeDtypeStruct((TOP_K, m), jnp.int32),
                   jax.ShapeDtypeStruct((TOP_K, m), F32),
                   jax.ShapeDtypeStruct((TOP_K, m), jnp.int32),
                   jax.ShapeDtypeStruct((N_EXPERTS, tile), F32)),
        grid=(m // tile,),
        in_specs=[pl.BlockSpec((N_EXPERTS, tile), col),
                  pl.BlockSpec((N_EXPERTS, tile), const),
                  pl.BlockSpec((tile, 2 * tile), const)],
        out_specs=(pl.BlockSpec((TOP_K, tile), col),
                   pl.BlockSpec((TOP_K, tile), col),
                   pl.BlockSpec((TOP_K, tile), col),
                   pl.BlockSpec((N_EXPERTS, tile), const)),
        compiler_params=_params("arbitrary"),
        name="moe_route",
    )(logits_t, bias, uo)
    return idx, gate, rank, cnt[:, 0].astype(jnp.int32)


def _dispatch(idx, rank, counts):
    m = idx.shape[1]
    r = MOE_ROWS
    padded = (counts + r - 1) // r * r
    pend = jnp.cumsum(padded)
    pstart = pend - padded
    experts = jnp.arange(N_EXPERTS, dtype=jnp.int32)
    dest = jnp.sum(jnp.where(idx[:, :, None] == experts, pstart.astype(jnp.int32), 0), axis=-1) + rank
    nb = -(-(m * TOP_K) // r) + N_EXPERTS
    blk_lo = jnp.arange(nb, dtype=jnp.int32) * r
    blk_e = jnp.minimum(jnp.sum(pend[None, :] <= blk_lo[:, None], axis=1), N_EXPERTS - 1).astype(jnp.int32)
    nvalid = jnp.clip((pstart + counts)[blk_e] - blk_lo, 0, r).astype(jnp.int32)
    pad_blk = jnp.concatenate([jnp.maximum(pend // r - 1, 0), pend[-1:] // r]).astype(jnp.int32)
    return dest, blk_e, nvalid, pad_blk


def _scatter_kernel(last_blk_ref, dest_ref, xa_ref, xb_ref, xg_ref, zero_ref, sem, zsem, *, tile, steps_a):
    i = pl.program_id(0)

    @pl.when(i == 0)
    def _():
        zero_ref[...] = jnp.zeros_like(zero_ref)

        def zero_copy(blk):
            start = pl.multiple_of(blk * MOE_ROWS, MOE_ROWS)
            return pltpu.make_async_copy(zero_ref, xg_ref.at[pl.ds(start, MOE_ROWS), :], zsem)

        def start(e, carry):
            zero_copy(last_blk_ref[e]).start()
            return carry

        def wait(e, carry):
            zero_copy(last_blk_ref[e]).wait()
            return carry

        lax.fori_loop(0, N_EXPERTS, start, 0)
        lax.fori_loop(0, N_EXPERTS, wait, 0)

        def start_tail(blk, carry):
            zero_copy(blk).start()
            return carry

        def wait_tail(blk, carry):
            zero_copy(blk).wait()
            return carry

        used = last_blk_ref[N_EXPERTS]
        lax.fori_loop(used, xg_ref.shape[0] // MOE_ROWS, start_tail, 0)
        lax.fori_loop(used, xg_ref.shape[0] // MOE_ROWS, wait_tail, 0)

    def push(x_ref):
        def body(t, carry):
            for k in range(TOP_K):
                pltpu.make_async_copy(x_ref.at[pl.ds(t, 1), :], xg_ref.at[pl.ds(dest_ref[k, t], 1), :],
                                      sem).start(priority=k % 2)
            return carry
        lax.fori_loop(0, tile, body, 0)
        for k in range(TOP_K):
            pltpu.make_async_copy(x_ref, xg_ref.at[pl.ds(0, tile), :], sem).wait()

    @pl.when(i < steps_a)
    def _():
        push(xa_ref)

    @pl.when(i >= steps_a)
    def _():
        push(xb_ref)


def _moe_scatter(last_blk, dest, x_a, x_b, cap):
    tile = ROW_TILE
    width = x_a.shape[1]
    steps_a, steps_b = x_a.shape[0] // tile, x_b.shape[0] // tile
    return pl.pallas_call(
        functools.partial(_scatter_kernel, tile=tile, steps_a=steps_a),
        out_shape=jax.ShapeDtypeStruct((cap, width), x_a.dtype),
        grid=(steps_a + steps_b,),
        in_specs=[pl.BlockSpec(memory_space=pltpu.SMEM),
                  pl.BlockSpec((None, TOP_K, tile), lambda i: (i, 0, 0), memory_space=pltpu.SMEM),
                  pl.BlockSpec((tile, width), lambda i: (jnp.minimum(i, steps_a - 1), 0)),
                  pl.BlockSpec((tile, width), lambda i: (jnp.maximum(i - steps_a, 0), 0))],
        out_specs=pl.BlockSpec(memory_space=pl.ANY),
        scratch_shapes=[pltpu.VMEM((MOE_ROWS, width), x_a.dtype), pltpu.SemaphoreType.DMA(()),
                        pltpu.SemaphoreType.DMA(())],
        compiler_params=_params("arbitrary"),
        name="moe_scatter",
    )(last_blk, dest, x_a, x_b)


def _moe_kernel(blk_e_ref, nvalid_ref, x_ref, wg_ref, wu_ref, wd_ref, o_ref, wgb, wub, wdb):
    i = pl.program_id(0)
    e = blk_e_ref[i]
    prev = blk_e_ref[jnp.maximum(i - 1, 0)]

    @pl.when((i == 0) | (e != prev))
    def _():
        wgb[...] = wg_ref[...].astype(BF16)
        wub[...] = wu_ref[...].astype(BF16)
        wdb[...] = wd_ref[...].astype(BF16)

    @pl.when(nvalid_ref[i] > 0)
    def _():
        x = x_ref[...].astype(BF16)
        act = (_silu(_dot(x, wgb[...])) * _dot(x, wub[...])).astype(BF16)
        o_ref[...] = _dot(act, wdb[...])

    @pl.when(nvalid_ref[i] == 0)
    def _():
        o_ref[...] = jnp.zeros_like(o_ref)


def _moe_experts(blk_e, nvalid, xg, w_gate_e, w_up_e, w_down_e):
    cap = xg.shape[0]
    r = MOE_ROWS
    grid_spec = pltpu.PrefetchScalarGridSpec(
        num_scalar_prefetch=2,
        grid=(cap // r,),
        in_specs=[pl.BlockSpec((r, D_MODEL), lambda i, be, nv: (i, 0)),
                  pl.BlockSpec((None, D_MODEL, D_EXPERT), lambda i, be, nv: (be[i], 0, 0)),
                  pl.BlockSpec((None, D_MODEL, D_EXPERT), lambda i, be, nv: (be[i], 0, 0)),
                  pl.BlockSpec((None, D_EXPERT, D_MODEL), lambda i, be, nv: (be[i], 0, 0))],
        out_specs=pl.BlockSpec((r, D_MODEL), lambda i, be, nv: (i, 0)),
        scratch_shapes=[pltpu.VMEM((D_MODEL, D_EXPERT), BF16),
                        pltpu.VMEM((D_MODEL, D_EXPERT), BF16),
                        pltpu.VMEM((D_EXPERT, D_MODEL), BF16)])
    return pl.pallas_call(
        _moe_kernel,
        out_shape=jax.ShapeDtypeStruct((cap, D_MODEL), F32),
        grid_spec=grid_spec,
        compiler_params=_params("arbitrary"),
        name="moe_experts",
    )(blk_e, nvalid, xg, w_gate_e, w_up_e, w_down_e)


def _final_kernel(dcur_ref, dnext_ref, x1_ref, gate_ref, rows_ref, wg_ref, wu_ref, wd_ref, g_ref, b_ref,
                  o_ref, buf, sem, *, tile):
    i = pl.program_id(0)
    n = pl.num_programs(0)
    slot = i % 2

    def gather(dest_ref, s):
        def body(t, carry):
            for k in range(TOP_K):
                pltpu.make_async_copy(rows_ref.at[pl.ds(dest_ref[k, t], 1), :],
                                      buf.at[s, k, pl.ds(t, 1), :], sem.at[s]).start(priority=k % 2)
            return carry
        lax.fori_loop(0, tile, body, 0)

    @pl.when(i == 0)
    def _():
        gather(dcur_ref, 0)

    @pl.when(i + 1 < n)
    def _():
        gather(dnext_ref, 1 - slot)

    x1 = x1_ref[...]
    xb = x1.astype(BF16)
    act = (_silu(_dot(xb, wg_ref[...])) * _dot(xb, wu_ref[...])).astype(BF16)
    moe = _dot(act, wd_ref[...])
    for k in range(TOP_K):
        pltpu.make_async_copy(rows_ref.at[pl.ds(0, tile), :], buf.at[slot, k], sem.at[slot]).wait()
    gate = gate_ref[...]
    for k in range(TOP_K):
        moe = moe + buf[slot, k] * gate[:, k:k + 1]
    o_ref[...] = _layer_norm(ALPHA * x1 + moe, g_ref[...], b_ref[...])


def _final(x1, dest, gates_t, rows, row_off, w_gate_s, w_up_s, w_down_s, ln_g, ln_b):
    m = x1.shape[0]
    tile = ROW_TILE
    n = m // tile
    off = row_off // tile
    row = lambda i: (i, 0)
    const = lambda i: (0, 0)
    return pl.pallas_call(
        functools.partial(_final_kernel, tile=tile),
        out_shape=jax.ShapeDtypeStruct((m, D_MODEL), F32),
        grid=(n,),
        in_specs=[pl.BlockSpec((None, TOP_K, tile), lambda i: (off + i, 0, 0), memory_space=pltpu.SMEM),
                  pl.BlockSpec((None, TOP_K, tile), lambda i: (off + jnp.minimum(i + 1, n - 1), 0, 0),
                               memory_space=pltpu.SMEM),
                  pl.BlockSpec((tile, D_MODEL), row),
                  pl.BlockSpec((tile, TOP_K), lambda i: (off + i, 0)),
                  pl.BlockSpec(memory_space=pl.ANY),
                  pl.BlockSpec((D_MODEL, D_EXPERT), const),
                  pl.BlockSpec((D_MODEL, D_EXPERT), const),
                  pl.BlockSpec((D_EXPERT, D_MODEL), const),
                  pl.BlockSpec((1, D_MODEL), const),
                  pl.BlockSpec((1, D_MODEL), const)],
        out_specs=pl.BlockSpec((tile, D_MODEL), row),
        scratch_shapes=[pltpu.VMEM((2, TOP_K, tile, D_MODEL), F32), pltpu.SemaphoreType.DMA((2,))],
        compiler_params=_params("arbitrary"),
        name="shared_ln",
    )(dest, dest, x1, gates_t, rows, w_gate_s, w_up_s, w_down_s, ln_g[None, :], ln_b[None, :])


def kernel(x_prompt, x_sample, cache_k, cache_v, state_conv, state_ssm, page_table, w_in, sb_bias, conv_w, conv_b, dt_bias, a_log, d_skip, g_attn, g_ssd, w_out, ln1_g, ln1_b, w_router, router_bias, w_gate_e, w_up_e, w_down_e, w_gate_s, w_up_s, w_down_s, ln2_g, ln2_b):
    assert w_in.shape[0] == 1, "one layer"
    bp, seq, _ = x_prompt.shape
    bs, dec, _ = x_sample.shape
    assert dec >= CONV_WIDTH - 1
    mp, ms = bp * seq, bs * dec
    n_phys = cache_k.shape[1]

    w_in_b = jnp.pad(w_in[0], ((0, 0), (0, IN_PAD - w_in.shape[2]))).astype(BF16)
    w_out_b = w_out[0].astype(BF16)
    w_router_t = w_router[0].T.astype(BF16)
    mix_w = (g_attn[0], g_ssd[0], w_out_b, ln1_g[0], ln1_b[0], w_router_t)
    ssd_w = (conv_w[0], conv_b[0], dt_bias[0], a_log[0], d_skip[0])
    dt_blk = IN_MAIN // LANES
    xbc_blk = (3 * ATTN_WIDTH + SSM_WIDTH) // CONV_DIM

    xp = x_prompt.reshape(mp, D_MODEL)
    proj_p = _proj(xp, w_in_b, 1024, 896)
    proj_p3 = proj_p.reshape(bp, seq, IN_PAD)
    k_prompt = proj_p3[:, :, ATTN_WIDTH:2 * ATTN_WIDTH].reshape(1, bp, seq, N_HEADS, HEAD_DIM)
    v_prompt = proj_p3[:, :, 2 * ATTN_WIDTH:3 * ATTN_WIDTH].reshape(1, bp, seq, N_HEADS, HEAD_DIM)
    conv_prompt = proj_p3[:, seq - (CONV_WIDTH - 1):, IN_MAIN - CONV_DIM:IN_MAIN][None]
    oa_p = _attn_prompt(proj_p, sb_bias[0], bp, seq)
    dtr_p = proj_p3[:, :, IN_MAIN:IN_MAIN + N_SSM_HEADS].transpose(0, 2, 1)
    y_p, h_p = _ssd(proj_p3, xbc_blk, jnp.zeros((bp, SUBLANES, CONV_DIM), F32),
                    jnp.zeros((bp, N_SSM_HEADS, SSM_HEAD_DIM, SSM_STATE), F32),
                    proj_p3, dt_blk, dtr_p, *ssd_w, rows_in=SSD_CHUNK, valid=SSD_CHUNK)
    x1_p, lg_p = _mix_out(oa_p, y_p.reshape(mp, SSM_WIDTH), proj_p, xp, *mix_w, tm=256)

    xs = x_sample.reshape(ms, D_MODEL)
    proj_s = _proj(xs, w_in_b, ms, 896)
    proj_s3 = proj_s.reshape(bs, dec, IN_PAD)
    k_sample = proj_s3[:, :, ATTN_WIDTH:2 * ATTN_WIDTH].reshape(1, bs, dec, N_HEADS, HEAD_DIM)
    v_sample = proj_s3[:, :, 2 * ATTN_WIDTH:3 * ATTN_WIDTH].reshape(1, bs, dec, N_HEADS, HEAD_DIM)
    conv_sample = proj_s3[:, dec - (CONV_WIDTH - 1):, IN_MAIN - CONV_DIM:IN_MAIN][None]
    proj_s8 = jnp.pad(proj_s3, ((0, 0), (0, SUBLANES - dec), (0, 0)))
    oa_s = _attn_sample(proj_s8, cache_k.reshape(n_phys, PAGE_SIZE * N_HEADS, HEAD_DIM),
                        cache_v.reshape(n_phys, PAGE_SIZE * N_HEADS, HEAD_DIM), page_table, sb_bias[0])
    dtc_s = jnp.pad(proj_s8[:, :, IN_MAIN:], ((0, 0), (0, SSD_CHUNK - SUBLANES), (0, 0)))
    dtr_s = dtc_s[:, :, :N_SSM_HEADS].transpose(0, 2, 1)
    conv0_s = jnp.pad(state_conv[0], ((0, 0), (SUBLANES - (CONV_WIDTH - 1), 0), (0, 0)))
    y_s, h_s = _ssd(proj_s8, xbc_blk, conv0_s, state_ssm[0], dtc_s, 0, dtr_s, *ssd_w,
                    rows_in=SUBLANES, valid=dec)
    x1_s, lg_s = _mix_out(oa_s[:, :dec].reshape(ms, ATTN_WIDTH), y_s[:, :dec].reshape(ms, SSM_WIDTH),
                                 proj_s, xs, *mix_w, tm=ms)

    m = mp + ms
    idx, gates, rank, counts = _route(jnp.concatenate([lg_p, lg_s], axis=1), router_bias[0])
    dest, blk_e, nvalid, pad_blk = _dispatch(idx, rank, counts)
    dest_t = dest.reshape(TOP_K, m // ROW_TILE, ROW_TILE).transpose(1, 0, 2)
    xg = _moe_scatter(pad_blk, dest_t, x1_p, x1_s, blk_e.shape[0] * MOE_ROWS)
    out_rows = _moe_experts(blk_e, nvalid, xg, w_gate_e[0], w_up_e[0], w_down_e[0])
    shared_w = (w_gate_s[0].astype(BF16), w_up_s[0].astype(BF16), w_down_s[0].astype(BF16), ln2_g[0], ln2_b[0])
    y_p = _final(x1_p, dest_t, gates.T, out_rows, 0, *shared_w)
    y_s = _final(x1_s, dest_t, gates.T, out_rows, mp, *shared_w)

    return (y_p.reshape(bp, seq, D_MODEL), y_s.reshape(bs, dec, D_MODEL),
            k_prompt, v_prompt, conv_prompt, h_p[None],
            k_sample, v_sample, conv_sample, h_s[None])
```
